```python
import math
import jax, jax.numpy as jnp
from jax import lax
import numpy as np

D_MODEL = 2048
BATCH = 4
SEQ = 2048
DEPTH = 2
DEC_BATCH = 128
DEC_SEQ = 1
PAST_LEN = 16384
PAGE_SIZE = 128

N_EVEN = (DEPTH + 1) // 2
N_ODD = DEPTH // 2
D_A = D_MODEL // 2
DK_A = 128
DV_A = 128
H_A = D_A // DV_A
GLA_CHUNK = 32
D_B = D_MODEL - D_A
POOL_WINDOWS = (2, 4, 8, 16)
N_POOL_GROUPS = len(POOL_WINDOWS)
G_B = D_B // N_POOL_GROUPS
POOL_HIST = max(POOL_WINDOWS) - 1
D_C = D_MODEL
C_CHUNK = 128
H_C = 8
G_C = D_C // H_C
N_EXPERTS = 64
TOP_K = 8
N_GROUPS = 8
TOPK_GROUPS = 4
D_EXPERT = 512
D_SHARED = 512
ROUTED_SCALE = 2.5
MOE_BLOCK = 128
ALPHA = (2.0 * DEPTH) ** 0.25
BETA = (8.0 * DEPTH) ** -0.25
LN_EPS = 1e-5
RMS_EPS = 1e-6
N_MOD = 6

kernel_name = 'hgrn2_pool_gmlp_moe_deepnorm_adaln_step'

F32 = jnp.float32


def _layer_norm(x, g, b):
    xf = x.astype(F32)
    mu = jnp.mean(xf, axis=-1, keepdims=True)
    var = jnp.mean(jnp.square(xf - mu), axis=-1, keepdims=True)
    return ((xf - mu) * lax.rsqrt(var + LN_EPS) * g.astype(F32) + b.astype(F32)).astype(x.dtype)


def _adaln(c, w, b):
    mod = jax.nn.silu(c) @ w + b
    return [m[:, None, :] for m in jnp.split(mod, N_MOD, axis=-1)]


def _gla_chunked(q, k, v, logf, s0):
    B, T = q.shape[0], q.shape[1]
    Tp = -(-T // GLA_CHUNK) * GLA_CHUNK
    pad = ((0, 0), (0, Tp - T), (0, 0), (0, 0))
    q, k, v, logf = [jnp.pad(a.astype(F32), pad) for a in (q, k, v, logf)]
    nc = Tp // GLA_CHUNK

    def to_chunks(a):
        return a.reshape(B, nc, GLA_CHUNK, H_A, a.shape[-1]).transpose(1, 0, 3, 2, 4)

    qc, kc, vc, gc = map(to_chunks, (q, k, v, logf))
    mask = jnp.tril(jnp.ones((GLA_CHUNK, GLA_CHUNK), dtype=bool))

    def step(S, xs):
        qb, kb, vb, gb = xs
        bcum = jnp.cumsum(gb, axis=2)
        blast = bcum[:, :, -1:, :]
        qe = qb * jnp.exp(bcum)
        ke = kb * jnp.exp(-bcum)
        att = jnp.where(mask, jnp.einsum('bhtk,bhsk->bhts', qe, ke), 0.0)
        o = jnp.einsum('bhts,bhsv->bhtv', att, vb) + jnp.einsum('bhtk,bhkv->bhtv', qe, S)
        kd = kb * jnp.exp(blast - bcum)
        S = jnp.exp(blast[:, :, 0, :])[..., None] * S + jnp.einsum('bhsk,bhsv->bhkv', kd, vb)
        return S, o

    S, o = lax.scan(step, s0.astype(F32), (qc, kc, vc, gc))
    o = o.transpose(1, 0, 3, 2, 4).reshape(B, Tp, H_A, DV_A)[:, :T]
    return o, S


def _hgrn2_mixer(q, f_pre, i_in, g_pre, lb, rms_w, s0):
    B, T, _ = q.shape
    fp = f_pre.astype(F32)
    logf = jnp.log(lb + (1.0 - lb) * jax.nn.sigmoid(fp))
    k = (1.0 - lb) * jax.nn.sigmoid(-fp)
    heads = lambda a, d: a.reshape(B, T, H_A, d)
    o, S = _gla_chunked(heads(q, DK_A), heads(k, DK_A), heads(i_in, DV_A), heads(logf, DK_A), s0)
    o = o * lax.rsqrt(jnp.mean(jnp.square(o), axis=-1, keepdims=True) + RMS_EPS)
    o = o.reshape(B, T, D_A) * rms_w.astype(F32) * jax.nn.silu(g_pre.astype(F32))
    return o.astype(q.dtype), S.astype(s0.dtype)


def _pool_mixer(u, hist, start_pos, pool_w, pool_scale):
    B, T, _ = u.shape
    ext = jnp.concatenate([hist.astype(u.dtype), u], axis=1)
    cs = jnp.concatenate([jnp.zeros((B, 1, D_B), F32), jnp.cumsum(ext.astype(F32), axis=1)], axis=1)
    pos = start_pos + jnp.arange(T)
    outs = []
    for g, w in enumerate(POOL_WINDOWS):
        sl = slice(g * G_B, (g + 1) * G_B)
        hi = cs[:, POOL_HIST + 1:POOL_HIST + 1 + T, sl]
        lo = cs[:, POOL_HIST + 1 - w:POOL_HIST + 1 - w + T, sl]
        cnt = jnp.minimum(pos + 1, w).astype(F32)[None, :, None]
        outs.append((hi - lo) / cnt)
    pooled = jnp.concatenate(outs, axis=-1) - u.astype(F32)
    mixed = jnp.einsum('btgc,gcd->btgd', pooled.reshape(B, T, N_POOL_GROUPS, G_B), pool_w.astype(F32))
    out = mixed.reshape(B, T, D_B) * pool_scale.astype(F32)
    return out.astype(u.dtype), ext[:, -POOL_HIST:]


def _even_mixer(h, w_in, w_out, lb, rms_w, pool_w, pool_scale, s0, hist, start_pos):
    z = h @ w_in
    q, f_pre, i_in, g_pre, u = jnp.split(z, [D_A, 2 * D_A, 3 * D_A, 4 * D_A], axis=-1)
    o_a, S = _hgrn2_mixer(q, f_pre, i_in, g_pre, lb, rms_w, s0)
    o_b, buf = _pool_mixer(u, hist, start_pos, pool_w, pool_scale)
    out = jnp.concatenate([o_a, o_b], axis=-1) @ w_out
    return out, S, buf


def _odd_mixer(h, w_in, ln_g, ln_b, w_sp, b_sp, w_out):
    B, T, _ = h.shape
    z = jax.nn.gelu(h @ w_in)
    u, v = jnp.split(z, 2, axis=-1)
    v = _layer_norm(v, ln_g, ln_b)
    Tp = -(-T // C_CHUNK) * C_CHUNK
    vc = jnp.pad(v, ((0, 0), (0, Tp - T), (0, 0))).reshape(B, Tp // C_CHUNK, C_CHUNK, H_C, G_C)
    w_causal = jnp.where(jnp.tril(jnp.ones((C_CHUNK, C_CHUNK), dtype=bool)), w_sp, 0.0)
    mixed = jnp.einsum('hts,bnshc->bnthc', w_causal, vc) + b_sp.T[None, None, :, :, None]
    mixed = mixed.reshape(B, Tp, D_C)[:, :T]
    out = (u * mixed) @ w_out
    return out, v


def _moe(x, w_router, router_bias, w_gate, w_up, w_down, ws_gate, ws_up, ws_down):
    B, T, D = x.shape
    xt = x.reshape(-1, D)
    N = xt.shape[0]
    scores = jax.nn.sigmoid(xt.astype(F32) @ w_router.astype(F32))
    sel = scores + router_bias.astype(F32)
    grp_score = lax.top_k(sel.reshape(N, N_GROUPS, N_EXPERTS // N_GROUPS), 2)[0].sum(-1)
    _, gidx = lax.top_k(grp_score, TOPK_GROUPS)
    gmask = jnp.any(gidx[..., None] == jnp.arange(N_GROUPS), axis=1)
    emask = jnp.repeat(gmask, N_EXPERTS // N_GROUPS, axis=1)
    _, eidx = lax.top_k(jnp.where(emask, sel, -jnp.inf), TOP_K)
    wsel = jnp.take_along_axis(scores, eidx, axis=1)
    wsel = wsel / jnp.sum(wsel, axis=-1, keepdims=True) * ROUTED_SCALE
    NK = N * TOP_K
    flat_e = eidx.reshape(-1)
    flat_tok = jnp.repeat(jnp.arange(N, dtype=jnp.int32), TOP_K)
    flat_w = wsel.reshape(-1)
    order = jnp.argsort(flat_e)
    e_s, tok_s, w_s = flat_e[order], flat_tok[order], flat_w[order]
    counts = jnp.bincount(flat_e, length=N_EXPERTS)
    starts = jnp.cumsum(counts) - counts
    pcounts = (counts + MOE_BLOCK - 1) // MOE_BLOCK * MOE_BLOCK
    pends = jnp.cumsum(pcounts)
    pstarts = pends - pcounts
    dest = pstarts[e_s] + (jnp.arange(NK) - starts[e_s])
    n_blocks = -(-NK // MOE_BLOCK) + N_EXPERTS
    P = n_blocks * MOE_BLOCK
    row_tok = jnp.full((P,), N, dtype=jnp.int32).at[dest].set(tok_s)
    row_w = jnp.zeros((P,), F32).at[dest].set(w_s)
    block_e = jnp.minimum(jnp.searchsorted(pends, jnp.arange(n_blocks) * MOE_BLOCK, side='right'), N_EXPERTS - 1)
    x_pad = jnp.concatenate([xt, jnp.zeros((1, D), xt.dtype)], axis=0)

    def block_fn(args):
        toks, wts, e = args
        xb = x_pad[toks]
        hb = jax.nn.silu(xb @ w_gate[e]) * (xb @ w_up[e])
        return (hb @ w_down[e]).astype(F32) * wts[:, None]

    yb = lax.map(block_fn, (row_tok.reshape(n_blocks, MOE_BLOCK), row_w.reshape(n_blocks, MOE_BLOCK), block_e))
    routed = jnp.zeros((N + 1, D), F32).at[row_tok].add(yb.reshape(P, D))[:N]
    shared = (jax.nn.silu(xt @ ws_gate) * (xt @ ws_up)) @ ws_down
    return (routed + shared.astype(F32)).reshape(B, T, D).astype(x.dtype)


def _trunk(x, c, start_pos, s_hgrn, s_pool,
           mix0_w_in, mix0_w_out, hgrn_lb_table, hgrn_rms_w, pool_w, pool_scale,
           mix1_w_in, gmlp_ln_g, gmlp_ln_b, gmlp_w_sp, gmlp_b_sp, mix1_w_out,
           ada_w, ada_b, ln_mix_g, ln_mix_b, ln_ffn_g, ln_ffn_b,
           moe_w_router, moe_router_bias, moe_w_gate, moe_w_up, moe_w_down,
           shared_w_gate, shared_w_up, shared_w_down):
    lb_all = jnp.cumsum(jax.nn.softmax(hgrn_lb_table.astype(F32), axis=0), axis=0)
    new_hgrn, new_pool, new_v = [], [], []
    for l in range(DEPTH):
        sh1, sc1, g1, sh2, sc2, g2 = _adaln(c, ada_w[l], ada_b[l])
        h = x * (1.0 + sc1) + sh1
        if l % 2 == 0:
            e = l // 2
            out, S, buf = _even_mixer(h, mix0_w_in[e], mix0_w_out[e], lb_all[l], hgrn_rms_w[e],
                                      pool_w[e], pool_scale[e], s_hgrn[e], s_pool[e], start_pos)
            new_hgrn.append(S)
            new_pool.append(buf)
        else:
            o = l // 2
            out, v = _odd_mixer(h, mix1_w_in[o], gmlp_ln_g[o], gmlp_ln_b[o], gmlp_w_sp[o], gmlp_b_sp[o],
                                mix1_w_out[o])
            new_v.append(v)
        x = _layer_norm(ALPHA * x + g1 * out, ln_mix_g[l], ln_mix_b[l])
        h = x * (1.0 + sc2) + sh2
        ffn = _moe(h, moe_w_router[l], moe_router_bias[l], moe_w_gate[l], moe_w_up[l], moe_w_down[l],
                   shared_w_gate[l], shared_w_up[l], shared_w_down[l])
        x = _layer_norm(ALPHA * x + g2 * ffn, ln_ffn_g[l], ln_ffn_b[l])
    return x, jnp.stack(new_hgrn), jnp.stack(new_pool), jnp.stack(new_v)


def setup_inputs(seed: int = 0) -> dict:
    key = jax.random.key(seed)
    ks = iter(jax.random.split(key, 40))
    nrm = lambda shape, s: jax.random.normal(next(ks), shape, F32) * s
    D = D_MODEL
    return {
        'x_prompt': nrm((BATCH, SEQ, D), 1.0),
        'x_sample': nrm((DEC_BATCH, DEC_SEQ, D), 1.0),
        'state_hgrn': nrm((N_EVEN, DEC_BATCH, H_A, DK_A, DV_A), 0.5),
        'state_pool': nrm((N_EVEN, DEC_BATCH, POOL_HIST, D_B), 1.0),
        'c_prompt': nrm((BATCH, D), 1.0),
        'c_sample': nrm((DEC_BATCH, D), 1.0),
        'mix0_w_in': nrm((N_EVEN, D, 4 * D_A + D_B), D ** -0.5),
        'mix0_w_out': nrm((N_EVEN, D_A + D_B, D), (D_A + D_B) ** -0.5 * BETA),
        'hgrn_lb_table': nrm((DEPTH + 1, D_A), 0.3),
        'hgrn_rms_w': 1.0 + nrm((N_EVEN, D_A), 0.05),
        'pool_w': nrm((N_EVEN, N_POOL_GROUPS, G_B, G_B), G_B ** -0.5),
        'pool_scale': 1.0 + nrm((N_EVEN, D_B), 0.1),
        'mix1_w_in': nrm((N_ODD, D, 2 * D_C), D ** -0.5),
        'gmlp_ln_g': 1.0 + nrm((N_ODD, D_C), 0.05),
        'gmlp_ln_b': nrm((N_ODD, D_C), 0.02),
        'gmlp_w_sp': nrm((N_ODD, H_C, C_CHUNK, C_CHUNK), 0.5 * C_CHUNK ** -0.5),
        'gmlp_b_sp': 1.0 + nrm((N_ODD, H_C, C_CHUNK), 0.1),
        'mix1_w_out': nrm((N_ODD, D_C, D), D_C ** -0.5 * BETA),
        'ada_w': nrm((DEPTH, D, N_MOD * D), 0.5 * D ** -0.5),
        'ada_b': nrm((DEPTH, N_MOD * D), 0.02),
        'ln_mix_g': 1.0 + nrm((DEPTH, D), 0.05),
        'ln_mix_b': nrm((DEPTH, D), 0.02),
        'ln_ffn_g': 1.0 + nrm((DEPTH, D), 0.05),
        'ln_ffn_b': nrm((DEPTH, D), 0.02),
        'moe_w_router': nrm((DEPTH, D, N_EXPERTS), D ** -0.5),
        'moe_router_bias': nrm((DEPTH, N_EXPERTS), 0.01),
        'moe_w_gate': nrm((DEPTH, N_EXPERTS, D, D_EXPERT), D ** -0.5),
        'moe_w_up': nrm((DEPTH, N_EXPERTS, D, D_EXPERT), D ** -0.5),
        'moe_w_down': nrm((DEPTH, N_EXPERTS, D_EXPERT, D), D_EXPERT ** -0.5 * BETA),
        'shared_w_gate': nrm((DEPTH, D, D_SHARED), D ** -0.5),
        'shared_w_up': nrm((DEPTH, D, D_SHARED), D ** -0.5),
        'shared_w_down': nrm((DEPTH, D_SHARED, D), D_SHARED ** -0.5 * BETA),
    }


def reference(x_prompt, x_sample, state_hgrn, state_pool, c_prompt, c_sample,
              mix0_w_in, mix0_w_out, hgrn_lb_table, hgrn_rms_w, pool_w, pool_scale,
              mix1_w_in, gmlp_ln_g, gmlp_ln_b, gmlp_w_sp, gmlp_b_sp, mix1_w_out,
              ada_w, ada_b, ln_mix_g, ln_mix_b, ln_ffn_g, ln_ffn_b,
              moe_w_router, moe_router_bias, moe_w_gate, moe_w_up, moe_w_down,
              shared_w_gate, shared_w_up, shared_w_down):
    weights = (mix0_w_in, mix0_w_out, hgrn_lb_table, hgrn_rms_w, pool_w, pool_scale,
               mix1_w_in, gmlp_ln_g, gmlp_ln_b, gmlp_w_sp, gmlp_b_sp, mix1_w_out,
               ada_w, ada_b, ln_mix_g, ln_mix_b, ln_ffn_g, ln_ffn_b,
               moe_w_router, moe_router_bias, moe_w_gate, moe_w_up, moe_w_down,
               shared_w_gate, shared_w_up, shared_w_down)
    hgrn0 = jnp.zeros((N_EVEN, x_prompt.shape[0], H_A, DK_A, DV_A), state_hgrn.dtype)
    pool0 = jnp.zeros((N_EVEN, x_prompt.shape[0], POOL_HIST, D_B), x_prompt.dtype)
    y_prompt, hgrn_prompt, pool_prompt, _ = _trunk(x_prompt, c_prompt, 0, hgrn0, pool0, *weights)
    y_sample, hgrn_sample, pool_sample, chunk_v_sample = _trunk(x_sample, c_sample, PAST_LEN,
                                                               state_hgrn, state_pool, *weights)
    return (y_prompt, y_sample, hgrn_prompt, pool_prompt, hgrn_sample, pool_sample, chunk_v_sample)
```

```python
import functools

import jax
import jax.numpy as jnp
from jax import lax
from jax.experimental import pallas as pl
from jax.experimental.pallas import tpu as pltpu

F32 = jnp.float32
BF16 = jnp.bfloat16

PAST_LEN = 16384
GLA_CHUNK = 32
POOL_WINDOWS = (2, 4, 8, 16)
C_CHUNK = 128
TOP_K = 8
N_GROUPS = 8
TOPK_GROUPS = 4
ROUTED_SCALE = 2.5
LN_EPS = 1e-5
RMS_EPS = 1e-6
N_MOD = 6

TM = 256
TM_FINAL = 128
BM = 256
VMEM_LIMIT = 56 * 1024 * 1024


def _cparams(sem):
    return pltpu.CompilerParams(dimension_semantics=sem, vmem_limit_bytes=VMEM_LIMIT)


def _silu(x):
    return x * jax.nn.sigmoid(x)


def _layer_norm(y, g, b):
    mu = jnp.mean(y, axis=-1, keepdims=True)
    d = y - mu
    var = jnp.mean(d * d, axis=-1, keepdims=True)
    return d * lax.rsqrt(var + LN_EPS) * g + b


def _load_weight_bf16(w_hbm, w_bf, stage, sem, ck):
    n = w_hbm.shape[0] // ck

    def cp(c, slot):
        return pltpu.make_async_copy(w_hbm.at[pl.ds(pl.multiple_of(c * ck, ck), ck)], stage.at[slot], sem.at[slot])

    cp(0, 0).start()

    def body(c, carry):
        slot = c % 2

        @pl.when(c + 1 < n)
        def _():
            cp(c + 1, 1 - slot).start()

        cp(c, slot).wait()
        w_bf[pl.ds(pl.multiple_of(c * ck, ck), ck), :] = stage[slot].astype(BF16)
        return carry

    lax.fori_loop(0, n, body, 0)


def _mod_specs(l, j, d, tm, n_prompt_tiles, tiles_per_batch, nb):
    s = pl.BlockSpec((None, tm, d), lambda i: (l, jnp.maximum(i - n_prompt_tiles, 0), j))
    p = pl.BlockSpec((None, None, 1, d), lambda i: (l, jnp.minimum(i // tiles_per_batch, nb - 1), 0, j))
    return [s, p]


def _adaln_kernel(c_ref, w_ref, b_ref, os_ref, op_ref, *, ns_rows):
    c = c_ref[...]
    s = _silu(c).astype(BF16)
    r = jnp.dot(s, w_ref[...].astype(BF16), preferred_element_type=F32) + b_ref[...]
    os_ref[...] = r[:ns_rows]
    op_ref[...] = r[ns_rows:]


def _adaln(c_all, ada_w, ada_b, ns_rows):
    depth, d, n6 = ada_w.shape
    cm = c_all.shape[0]
    tn = 1024
    return pl.pallas_call(
        functools.partial(_adaln_kernel, ns_rows=ns_rows),
        grid=(depth, n6 // tn),
        in_specs=[pl.BlockSpec((cm, d), lambda l, j: (0, 0)),
                  pl.BlockSpec((None, d, tn), lambda l, j: (l, 0, j)),
                  pl.BlockSpec((None, 1, tn), lambda l, j: (l, 0, j))],
        out_specs=[pl.BlockSpec((None, ns_rows, tn), lambda l, j: (l, 0, j)),
                   pl.BlockSpec((None, cm - ns_rows, tn), lambda l, j: (l, 0, j))],
        out_shape=[jax.ShapeDtypeStruct((depth, ns_rows, n6), F32),
                   jax.ShapeDtypeStruct((depth, cm - ns_rows, n6), F32)],
        compiler_params=_cparams(("arbitrary", "arbitrary")),
        name="adaln",
    )(c_all, ada_w, ada_b.reshape(depth, 1, n6))


def _inproj_kernel(x_ref, scs_ref, scp_ref, shs_ref, shp_ref, w_hbm, *rest, n_prompt_tiles, nc, odd):
    if odd:
        lng_ref, lnb_ref, u_ref, v_ref, w_bf, hb, stage, sem = rest
    else:
        z_ref, w_bf, hb, stage, sem = rest
    i = pl.program_id(0)

    @pl.when(i == 0)
    def _():
        _load_weight_bf16(w_hbm, w_bf, stage, sem, stage.shape[1])

    is_s = i >= n_prompt_tiles
    sc = jnp.where(is_s, scs_ref[...], scp_ref[...])
    sh = jnp.where(is_s, shs_ref[...], shp_ref[...])
    hb[...] = (x_ref[...] * (1.0 + sc) + sh).astype(BF16)
    n = w_bf.shape[1]
    for c in range(n // nc):
        z = jnp.dot(hb[...], w_bf[:, c * nc:(c + 1) * nc], preferred_element_type=F32)
        if not odd:
            z_ref[:, c * nc:(c + 1) * nc] = z
        else:
            z = jax.nn.gelu(z)
            half = n // 2
            if c * nc < half:
                u_ref[:, c * nc:(c + 1) * nc] = z
            else:
                v_ref[:, c * nc - half:(c + 1) * nc - half] = z
    if odd:
        v_ref[...] = _layer_norm(v_ref[...], lng_ref[...], lnb_ref[...])


def _inproj(x_all, mod_s, mod_p, l, w, n_prompt_tiles, tiles_per_batch, nb, ln=None):
    ntp, d = x_all.shape
    n = w.shape[1]
    odd = ln is not None
    ck, nc = 128, 1024
    specs = [pl.BlockSpec((TM, d), lambda i: (i, 0))]
    specs += _mod_specs(l, 1, d, TM, n_prompt_tiles, tiles_per_batch, nb)
    specs += _mod_specs(l, 0, d, TM, n_prompt_tiles, tiles_per_batch, nb)
    specs += [pl.BlockSpec(memory_space=pl.ANY)]
    args = [x_all, mod_s, mod_p, mod_s, mod_p, w]
    if odd:
        specs += [pl.BlockSpec((1, n // 2), lambda i: (0, 0))] * 2
        args += [ln[0].reshape(1, -1), ln[1].reshape(1, -1)]
        out_specs = [pl.BlockSpec((TM, n // 2), lambda i: (i, 0))] * 2
        out_shape = [jax.ShapeDtypeStruct((ntp, n // 2), F32)] * 2
    else:
        out_specs = pl.BlockSpec((TM, n), lambda i: (i, 0))
        out_shape = jax.ShapeDtypeStruct((ntp, n), F32)
    return pl.pallas_call(
        functools.partial(_inproj_kernel, n_prompt_tiles=n_prompt_tiles, nc=nc, odd=odd),
        grid=(ntp // TM,),
        in_specs=specs, out_specs=out_specs, out_shape=out_shape,
        scratch_shapes=[pltpu.VMEM((d, n), BF16), pltpu.VMEM((TM, d), BF16),
                        pltpu.VMEM((2, ck, n), F32), pltpu.SemaphoreType.DMA((2,))],
        compiler_params=_cparams(("arbitrary",)),
        name="inproj_odd" if odd else "inproj_even",
    )(*args)


def _forget_lower_bound(rows, l):
    m = functools.reduce(jnp.maximum, rows)
    e = [jnp.exp(r - m) for r in rows]
    return sum(e[:l + 1]) / sum(e)


def _gla_prompt_kernel(q_ref, f_ref, v_ref, g_ref, lbt_ref, rms_ref, o_ref, s_ref,
                       st, qe_s, ke_s, kd_s, es_s, o_s, *, l, n_heads, dk):
    tb = pl.program_id(1)
    rows = q_ref.shape[0]
    nchunk = rows // GLA_CHUNK

    @pl.when(tb == 0)
    def _():
        st[...] = jnp.zeros_like(st)

    lb = _forget_lower_bound([lbt_ref[r:r + 1, :] for r in range(lbt_ref.shape[0])], l)
    fp = f_ref[...]
    logf = jnp.log(lb + (1.0 - lb) * jax.nn.sigmoid(fp))
    k = (1.0 - lb) * jax.nn.sigmoid(-fp)
    rid = lax.broadcasted_iota(jnp.int32, logf.shape, 0) % GLA_CHUNK
    bcum = logf
    s = 1
    while s < GLA_CHUNK:
        bcum = bcum + jnp.where(rid >= s, pltpu.roll(bcum, s, axis=0), 0.0)
        s *= 2
    b3 = bcum.reshape(nchunk, GLA_CHUNK, bcum.shape[1])
    blast = b3[:, GLA_CHUNK - 1:GLA_CHUNK, :]
    es_s[...] = jnp.exp(blast).reshape(nchunk, bcum.shape[1])
    blast_full = jnp.broadcast_to(blast, b3.shape).reshape(bcum.shape)
    qe_s[...] = q_ref[...] * jnp.exp(bcum)
    ke_s[...] = k * jnp.exp(-bcum)
    kd_s[...] = k * jnp.exp(blast_full - bcum)

    ri = lax.broadcasted_iota(jnp.int32, (GLA_CHUNK, GLA_CHUNK), 0)
    ci = lax.broadcasted_iota(jnp.int32, (GLA_CHUNK, GLA_CHUNK), 1)
    causal = ri >= ci

    def chunk_step(c, carry):
        r0 = pl.multiple_of(c * GLA_CHUNK, GLA_CHUNK)
        es_row = es_s[pl.ds(c, 1), :]
        for h in range(n_heads):
            cols = slice(h * dk, (h + 1) * dk)
            qe = qe_s[pl.ds(r0, GLA_CHUNK), cols]
            ke = ke_s[pl.ds(r0, GLA_CHUNK), cols]
            kd = kd_s[pl.ds(r0, GLA_CHUNK), cols]
            v = v_ref[pl.ds(r0, GLA_CHUNK), cols]
            att = lax.dot_general(qe, ke, (((1,), (1,)), ((), ())), preferred_element_type=F32)
            att = jnp.where(causal, att, 0.0)
            sth = st[h]
            o = jnp.dot(att, v, preferred_element_type=F32)
            o = o + lax.dot_general(qe, sth, (((1,), (1,)), ((), ())), preferred_element_type=F32)
            upd = lax.dot_general(v, kd, (((0,), (0,)), ((), ())), preferred_element_type=F32)
            st[h] = es_row[:, cols] * sth + upd
            o_s[pl.ds(r0, GLA_CHUNK), cols] = o
        return carry

    lax.fori_loop(0, nchunk, chunk_step, 0)

    for h in range(n_heads):
        cols = slice(h * dk, (h + 1) * dk)
        o = o_s[:, cols]
        o = o * lax.rsqrt(jnp.mean(o * o, axis=-1, keepdims=True) + RMS_EPS)
        o_ref[:, cols] = (o * rms_ref[:, cols] * _silu(g_ref[:, cols])).astype(o_ref.dtype)

    @pl.when(tb == pl.num_programs(1) - 1)
    def _():
        for h in range(n_heads):
            s_ref[0, h] = st[h].T


def _gla_prompt(z, lb_table, rms_w, l, nb, t, n_heads, dk):
    d_a = n_heads * dk
    tbk = 512
    nt = t // tbk
    row = lambda b, tb: b * nt + tb
    zspec = lambda sec: pl.BlockSpec((tbk, d_a), lambda b, tb: (row(b, tb), sec))
    return pl.pallas_call(
        functools.partial(_gla_prompt_kernel, l=l, n_heads=n_heads, dk=dk),
        grid=(nb, nt),
        in_specs=[zspec(0), zspec(1), zspec(2), zspec(3),
                  pl.BlockSpec(lb_table.shape, lambda b, tb: (0, 0)),
                  pl.BlockSpec((1, d_a), lambda b, tb: (0, 0))],
        out_specs=[pl.BlockSpec((tbk, d_a), lambda b, tb: (row(b, tb), 0)),
                   pl.BlockSpec((1, n_heads, dk, dk), lambda b, tb: (b, 0, 0, 0))],
        out_shape=[jax.ShapeDtypeStruct((nb * t, d_a), BF16),
                   jax.ShapeDtypeStruct((nb, n_heads, dk, dk), F32)],
        scratch_shapes=[pltpu.VMEM((n_heads, dk, dk), F32),
                        pltpu.VMEM((tbk, d_a), F32), pltpu.VMEM((tbk, d_a), F32), pltpu.VMEM((tbk, d_a), F32),
                        pltpu.VMEM((tbk // GLA_CHUNK, d_a), F32), pltpu.VMEM((tbk, d_a), F32)],
        compiler_params=_cparams(("arbitrary", "arbitrary")),
        name="gla_prompt",
    )(z, z, z, z, lb_table, rms_w.reshape(1, d_a))


def _gla_sample_kernel(qt_ref, ft_ref, lbt_ref, v_ref, g_ref, rms_ref, s0_ref, o_ref, s_ref, *, l):
    ns = s0_ref.shape[0]
    lbc = _forget_lower_bound([lbt_ref[r] for r in range(lbt_ref.shape[0])], l)
    fp = ft_ref[...]
    fcol = lbc + (1.0 - lbc) * jax.nn.sigmoid(fp)
    kcol = (1.0 - lbc) * jax.nn.sigmoid(-fp)
    qcol = qt_ref[...]
    for b in range(ns):
        v = v_ref[b:b + 1, :]
        s_new = fcol[:, b:b + 1] * s0_ref[b, 0] + kcol[:, b:b + 1] * v
        s_ref[b, 0] = s_new
        o_ref[b:b + 1, :] = jnp.sum(qcol[:, b:b + 1] * s_new, axis=0, keepdims=True)
    o = o_ref[...]
    o = o * lax.rsqrt(jnp.mean(o * o, axis=-1, keepdims=True) + RMS_EPS)
    o_ref[...] = o * rms_ref[...] * _silu(g_ref[...])


def _gla_sample(qt, ft, lbt_col, v_s, g_s, rms_w, s0, l):
    ns, n_heads, dk, dv = s0.shape
    return pl.pallas_call(
        functools.partial(_gla_sample_kernel, l=l),
        grid=(n_heads,),
        in_specs=[pl.BlockSpec((None, dk, ns), lambda h: (h, 0, 0)),
                  pl.BlockSpec((None, dk, ns), lambda h: (h, 0, 0)),
                  pl.BlockSpec((lbt_col.shape[0], None, dk, 1), lambda h: (0, h, 0, 0)),
                  pl.BlockSpec((ns, dv), lambda h: (0, h)),
                  pl.BlockSpec((ns, dv), lambda h: (0, h)),
                  pl.BlockSpec((1, dv), lambda h: (0, h)),
                  pl.BlockSpec((ns, 1, dk, dv), lambda h: (0, h, 0, 0))],
        out_specs=[pl.BlockSpec((ns, dv), lambda h: (0, h)),
                   pl.BlockSpec((ns, 1, dk, dv), lambda h: (0, h, 0, 0))],
        out_shape=[jax.ShapeDtypeStruct((ns, n_heads * dv), F32),
                   jax.ShapeDtypeStruct(s0.shape, F32)],
        compiler_params=_cparams(("arbitrary",)),
        name="gla_sample",
    )(qt, ft, lbt_col, v_s, g_s, rms_w.reshape(1, -1), s0)


def _pool_mix(win_sums, u, inv_cnt, pw_ref, ps_ref, o_ref, gb):
    for gi in range(len(POOL_WINDOWS)):
        cols = slice(gi * gb, (gi + 1) * gb)
        pooled = win_sums[gi] * inv_cnt[gi] - u[:, cols]
        mixed = jnp.dot(pooled.astype(BF16), pw_ref[gi].astype(BF16), preferred_element_type=F32)
        o_ref[:, cols] = (mixed * ps_ref[:, cols]).astype(o_ref.dtype)


def _pool_prompt_kernel(u_ref, pw_ref, ps_ref, o_ref, tail_ref, buf, *, hist_pad):
    tb = pl.program_id(1)
    rows, d_b = u_ref.shape
    gb = d_b // len(POOL_WINDOWS)

    @pl.when(tb == 0)
    def _():
        buf[0:hist_pad, :] = jnp.zeros((hist_pad, d_b), F32)

    u = u_ref[...]
    buf[hist_pad:, :] = u
    acc = buf[...]
    sums = []
    span = 1
    for gi, w in enumerate(POOL_WINDOWS):
        while span < w:
            acc = acc + pltpu.roll(acc, span, axis=0)
            span *= 2
        sums.append(acc[hist_pad:, gi * gb:(gi + 1) * gb])
    pos = tb * rows + lax.broadcasted_iota(jnp.int32, (rows, gb), 0)
    inv_cnt = [1.0 / jnp.minimum(pos + 1, w).astype(F32) for w in POOL_WINDOWS]
    _pool_mix(sums, u, inv_cnt, pw_ref, ps_ref, o_ref, gb)
    buf[0:hist_pad, :] = u[rows - hist_pad:, :]
    tail_ref[0] = u[rows - hist_pad:, :]


def _pool_prompt(z, pool_w, pool_scale, nb, t, d_b, u_col_block, hist_pad):
    tbk = 512
    nt = t // tbk
    return pl.pallas_call(
        functools.partial(_pool_prompt_kernel, hist_pad=hist_pad),
        grid=(nb, nt),
        in_specs=[pl.BlockSpec((tbk, d_b), lambda b, tb: (b * nt + tb, u_col_block)),
                  pl.BlockSpec(pool_w.shape, lambda b, tb: (0, 0, 0)),
                  pl.BlockSpec((1, d_b), lambda b, tb: (0, 0))],
        out_specs=[pl.BlockSpec((tbk, d_b), lambda b, tb: (b * nt + tb, 0)),
                   pl.BlockSpec((1, hist_pad, d_b), lambda b, tb: (b, 0, 0))],
        out_shape=[jax.ShapeDtypeStruct((nb * t, d_b), BF16),
                   jax.ShapeDtypeStruct((nb, hist_pad, d_b), F32)],
        scratch_shapes=[pltpu.VMEM((hist_pad + tbk, d_b), F32)],
        compiler_params=_cparams(("arbitrary", "arbitrary")),
        name="pool_prompt",
    )(z, pool_w, pool_scale.reshape(1, d_b))


def _pool_sample_kernel(hist_ref, u_ref, pw_ref, ps_ref, o_ref, nb_ref, *, n_hist):
    ns, d_b = u_ref.shape
    gb = d_b // len(POOL_WINDOWS)
    u = u_ref[...]
    row = lambda j: hist_ref[:, j * d_b:(j + 1) * d_b]
    sums, inv_cnt = [], []
    for gi, w in enumerate(POOL_WINDOWS):
        cols = slice(gi * gb, (gi + 1) * gb)
        acc = u[:, cols]
        for j in range(n_hist - (w - 1), n_hist):
            acc = acc + row(j)[:, cols]
        sums.append(acc)
        inv_cnt.append(1.0 / float(min(PAST_LEN + 1, w)))
    _pool_mix(sums, u, inv_cnt, pw_ref, ps_ref, o_ref, gb)
    nb_ref[:, 0:(n_hist - 1) * d_b] = hist_ref[:, d_b:]
    nb_ref[:, (n_hist - 1) * d_b:] = u


def _pool_sample(hist2d, z, pool_w, pool_scale, ns, d_b, u_row_block, u_col_block, n_hist):
    return pl.pallas_call(
        functools.partial(_pool_sample_kernel, n_hist=n_hist),
        grid=(1,),
        in_specs=[pl.BlockSpec(hist2d.shape, lambda i: (0, 0)),
                  pl.BlockSpec((ns, d_b), lambda i: (u_row_block, u_col_block)),
                  pl.BlockSpec(pool_w.shape, lambda i: (0, 0, 0)),
                  pl.BlockSpec((1, d_b), lambda i: (0, 0))],
        out_specs=[pl.BlockSpec((ns, d_b), lambda i: (0, 0)),
                   pl.BlockSpec(hist2d.shape, lambda i: (0, 0))],
        out_shape=[jax.ShapeDtypeStruct((ns, d_b), BF16),
                   jax.ShapeDtypeStruct(hist2d.shape, F32)],
        compiler_params=_cparams(("arbitrary",)),
        name="pool_sample",
    )(hist2d, z, pool_w, pool_scale.reshape(1, d_b))


def _spatial_kernel(u_ref, v_ref, wsp_ref, bt_ref, w00_ref, b0_ref, o_ref, *, n_prompt_tiles, n_heads):
    i = pl.program_id(0)
    rows, d_c = u_ref.shape
    gc = d_c // n_heads

    @pl.when(i < n_prompt_tiles)
    def _():
        ri = lax.broadcasted_iota(jnp.int32, (C_CHUNK, C_CHUNK), 0)
        ci = lax.broadcasted_iota(jnp.int32, (C_CHUNK, C_CHUNK), 1)
        for h in range(n_heads):
            wc = jnp.where(ri >= ci, wsp_ref[h], 0.0).astype(BF16)
            bias = bt_ref[:, h:h + 1]
            for c in range(rows // C_CHUNK):
                rs = slice(c * C_CHUNK, (c + 1) * C_CHUNK)
                cs = slice(h * gc, (h + 1) * gc)
                mixed = jnp.dot(wc, v_ref[rs, cs].astype(BF16), preferred_element_type=F32) + bias
                o_ref[rs, cs] = (u_ref[rs, cs] * mixed).astype(o_ref.dtype)

    @pl.when(i >= n_prompt_tiles)
    def _():
        o_ref[...] = (u_ref[...] * (v_ref[...] * w00_ref[...] + b0_ref[...])).astype(o_ref.dtype)


def _spatial(u, v, w_sp, b_sp, n_prompt_tiles):
    ntp, d_c = u.shape
    n_heads = w_sp.shape[0]
    gc = d_c // n_heads
    w00 = jnp.repeat(w_sp[:, 0, 0], gc).reshape(1, d_c)
    b0 = jnp.repeat(b_sp[:, 0], gc).reshape(1, d_c)
    tile = pl.BlockSpec((TM, d_c), lambda i: (i, 0))
    return pl.pallas_call(
        functools.partial(_spatial_kernel, n_prompt_tiles=n_prompt_tiles, n_heads=n_heads),
        grid=(ntp // TM,),
        in_specs=[tile, tile,
                  pl.BlockSpec(w_sp.shape, lambda i: (0, 0, 0)),
                  pl.BlockSpec((C_CHUNK, n_heads), lambda i: (0, 0)),
                  pl.BlockSpec((1, d_c), lambda i: (0, 0)),
                  pl.BlockSpec((1, d_c), lambda i: (0, 0))],
        out_specs=tile,
        out_shape=jax.ShapeDtypeStruct((ntp, d_c), BF16),
        compiler_params=_cparams(("arbitrary",)),
        name="spatial",
    )(u, v, w_sp, b_sp.T, w00, b0)


def _outproj_kernel(*refs, n_lhs, n_prompt_tiles, alpha):
    lhs = refs[:n_lhs]
    (x_ref, g1s, g1p, sc2s, sc2p, sh2s, sh2p, lng_ref, lnb_ref, wr_ref, w_hbm,
     x1_ref, h2_ref, lg_ref, w_bf, stage, sem) = refs[n_lhs:]
    i = pl.program_id(0)

    @pl.when(i == 0)
    def _():
        _load_weight_bf16(w_hbm, w_bf, stage, sem, stage.shape[1])

    is_s = i >= n_prompt_tiles
    out = None
    k0 = 0
    for a_ref in lhs:
        kk = a_ref.shape[1]
        part = jnp.dot(a_ref[...], w_bf[k0:k0 + kk, :], preferred_element_type=F32)
        out = part if out is None else out + part
        k0 += kk
    g1 = jnp.where(is_s, g1s[...], g1p[...])
    x1 = _layer_norm(alpha * x_ref[...] + g1 * out, lng_ref[...], lnb_ref[...])
    x1_ref[...] = x1
    sc2 = jnp.where(is_s, sc2s[...], sc2p[...])
    sh2 = jnp.where(is_s, sh2s[...], sh2p[...])
    h2 = x1 * (1.0 + sc2) + sh2
    h2_ref[...] = h2
    lg_ref[...] = lax.dot_general(wr_ref[...], h2, (((1,), (1,)), ((), ())),
                                  precision=lax.Precision.HIGHEST, preferred_element_type=F32)


def _outproj(lhs, x_all, mod_s, mod_p, l, w_out, ln_g, ln_b, w_router_t, n_prompt_tiles, tiles_per_batch, nb, alpha):
    ntp, d = x_all.shape
    n_e = w_router_t.shape[0]
    ck = 256
    tile = pl.BlockSpec((TM, d), lambda i: (i, 0))
    specs = [pl.BlockSpec((TM, a.shape[1]), lambda i: (i, 0)) for a in lhs] + [tile]
    for j in (2, 4, 3):
        specs += _mod_specs(l, j, d, TM, n_prompt_tiles, tiles_per_batch, nb)
    specs += [pl.BlockSpec((1, d), lambda i: (0, 0))] * 2
    specs += [pl.BlockSpec((n_e, d), lambda i: (0, 0)), pl.BlockSpec(memory_space=pl.ANY)]
    return pl.pallas_call(
        functools.partial(_outproj_kernel, n_lhs=len(lhs), n_prompt_tiles=n_prompt_tiles, alpha=alpha),
        grid=(ntp // TM,),
        in_specs=specs,
        out_specs=[tile, tile, pl.BlockSpec((n_e, TM), lambda i: (0, i))],
        out_shape=[jax.ShapeDtypeStruct((ntp, d), F32), jax.ShapeDtypeStruct((ntp, d), F32),
                   jax.ShapeDtypeStruct((n_e, ntp), F32)],
        scratch_shapes=[pltpu.VMEM(w_out.shape, BF16), pltpu.VMEM((2, ck, d), F32), pltpu.SemaphoreType.DMA((2,))],
        compiler_params=_cparams(("arbitrary",)),
        name="outproj",
    )(*lhs, x_all, *([mod_s, mod_p] * 3), ln_g.reshape(1, d), ln_b.reshape(1, d), w_router_t, w_out)


def _first_argmax(vals, iota, axis, size):
    m = jnp.max(vals, axis=axis, keepdims=True)
    idx = jnp.min(jnp.where(vals == m, iota, size), axis=axis, keepdims=True)
    return m, idx


def _route_kernel(lg_ref, bias_ref, e_ref, w_ref):
    n_e, tm = lg_ref.shape
    gsz = n_e // N_GROUPS
    neg = -jnp.inf
    scores = jax.nn.sigmoid(lg_ref[...])
    sel = scores + bias_ref[...]
    sel3 = sel.reshape(N_GROUPS, gsz, tm)
    io3 = lax.broadcasted_iota(jnp.int32, sel3.shape, 1)
    m1, i1 = _first_argmax(sel3, io3, 1, gsz)
    m2 = jnp.max(jnp.where(io3 == i1, neg, sel3), axis=1, keepdims=True)
    gscore = (m1 + m2).reshape(N_GROUPS, tm)
    iog = lax.broadcasted_iota(jnp.int32, gscore.shape, 0)
    gsel = jnp.zeros(gscore.shape, F32)
    cur = gscore
    for _ in range(TOPK_GROUPS):
        _, gi = _first_argmax(cur, iog, 0, N_GROUPS)
        pick = iog == gi
        gsel = jnp.where(pick, 1.0, gsel)
        cur = jnp.where(pick, neg, cur)
    esel = jnp.broadcast_to(gsel.reshape(N_GROUPS, 1, tm), sel3.shape).reshape(n_e, tm)
    ioe = lax.broadcasted_iota(jnp.int32, sel.shape, 0)
    iok = lax.broadcasted_iota(jnp.int32, (TOP_K, tm), 0)
    cur = jnp.where(esel > 0.5, sel, neg)
    eidx = jnp.zeros((TOP_K, tm), jnp.int32)
    sc = jnp.zeros((TOP_K, tm), F32)
    for r in range(TOP_K):
        _, ei = _first_argmax(cur, ioe, 0, n_e)
        pick = ioe == ei
        val = jnp.sum(jnp.where(pick, scores, 0.0), axis=0, keepdims=True)
        eidx = jnp.where(iok == r, ei, eidx)
        sc = jnp.where(iok == r, val, sc)
        cur = jnp.where(pick, neg, cur)
    e_ref[...] = eidx
    w_ref[...] = sc / jnp.sum(sc, axis=0, keepdims=True) * ROUTED_SCALE


def _route(logits_t, router_bias):
    n_e, ntp = logits_t.shape
    return pl.pallas_call(
        _route_kernel,
        grid=(ntp // TM,),
        in_specs=[pl.BlockSpec((n_e, TM), lambda i: (0, i)), pl.BlockSpec((n_e, 1), lambda i: (0, 0))],
        out_specs=[pl.BlockSpec((TOP_K, TM), lambda i: (0, i))] * 2,
        out_shape=[jax.ShapeDtypeStruct((TOP_K, ntp), jnp.int32), jax.ShapeDtypeStruct((TOP_K, ntp), F32)],
        compiler_params=_cparams(("arbitrary",)),
        name="route",
    )(logits_t, router_bias.reshape(n_e, 1))


def _dispatch_tables(eidx_t, n_e):
    flat_e = eidx_t.T.reshape(-1)
    nk = flat_e.shape[0]
    n_blocks = nk // BM + n_e
    order = jnp.argsort(flat_e).astype(jnp.int32)
    counts = jnp.bincount(flat_e, length=n_e).astype(jnp.int32)
    starts = jnp.cumsum(counts) - counts
    nblk = (counts + BM - 1) // BM
    bends = jnp.cumsum(nblk)
    total = bends[-1]
    g = jnp.arange(n_blocks, dtype=jnp.int32)
    e_of = jnp.minimum(jnp.searchsorted(bends, g, side='right'), n_e - 1).astype(jnp.int32)
    e_last = e_of[jnp.maximum(total - 1, 0)]
    used = g < total
    blk_e = jnp.where(used, e_of, e_last)
    within = g - (bends - nblk)[blk_e]
    blk_n = jnp.where(used, jnp.clip(counts[blk_e] - within * BM, 0, BM), 0).astype(jnp.int32)
    r = jnp.arange(BM, dtype=jnp.int32)
    src = starts[blk_e][:, None] + within[:, None] * BM + r[None, :]
    ids = order[jnp.clip(src, 0, nk - 1)]
    idx = jnp.where(r[None, :] < blk_n[:, None], ids, -1).reshape(-1)
    return blk_e, blk_n, idx


def _experts_kernel(blk_e, blk_n, idx, h2_hbm, wg_ref, wu_ref, wd_ref, y_hbm,
                    xbuf, ybuf, wg_bf, wu_bf, wd_bf, gsem, ssem, *, ntp):
    g = pl.program_id(0)
    n = blk_n[g]
    base = g * BM

    kbits = TOP_K.bit_length() - 1

    def gather(r):
        a = jnp.maximum(idx[base + r], 0)
        tok = lax.shift_right_logical(a, kbits)
        return pltpu.make_async_copy(h2_hbm.at[pl.ds(tok, 1)], xbuf.at[pl.ds(r, 1)], gsem)

    def scatter(r):
        a = idx[base + r]
        ac = jnp.maximum(a, 0)
        dst = jnp.bitwise_and(ac, TOP_K - 1) * ntp + lax.shift_right_logical(ac, kbits)
        return a, pltpu.make_async_copy(ybuf.at[pl.ds(r, 1)], y_hbm.at[pl.ds(dst, 1)], ssem)

    @pl.when(n > 0)
    def _():
        def issue(r, c):
            gather(r).start()
            return c
        lax.fori_loop(0, BM, issue, 0, unroll=8)

        prev = blk_e[jnp.maximum(g - 1, 0)]

        @pl.when(jnp.logical_or(g == 0, blk_e[g] != prev))
        def _():
            wg_bf[...] = wg_ref[...].astype(BF16)
            wu_bf[...] = wu_ref[...].astype(BF16)
            wd_bf[...] = wd_ref[...].astype(BF16)

        def wait_g(r, c):
            gather(r).wait()
            return c
        lax.fori_loop(0, BM, wait_g, 0, unroll=8)

        x = xbuf[...].astype(BF16)
        hg = jnp.dot(x, wg_bf[...], preferred_element_type=F32)
        hu = jnp.dot(x, wu_bf[...], preferred_element_type=F32)
        h = (_silu(hg) * hu).astype(BF16)
        ybuf[...] = jnp.dot(h, wd_bf[...], preferred_element_type=F32)

        def issue_s(r, c):
            a, cp = scatter(r)

            @pl.when(a >= 0)
            def _():
                cp.start()
            return c
        lax.fori_loop(0, BM, issue_s, 0, unroll=8)

        def wait_s(r, c):
            a, cp = scatter(r)

            @pl.when(a >= 0)
            def _():
                cp.wait()
            return c
        lax.fori_loop(0, BM, wait_s, 0, unroll=8)


def _experts(h2, blk_e, blk_n, idx, w_gate, w_up, w_down):
    ntp, d = h2.shape
    n_e, _, d_e = w_gate.shape
    n_blocks = blk_e.shape[0]
    return pl.pallas_call(
        functools.partial(_experts_kernel, ntp=ntp),
        grid_spec=pltpu.PrefetchScalarGridSpec(
            num_scalar_prefetch=3, grid=(n_blocks,),
            in_specs=[pl.BlockSpec(memory_space=pl.ANY),
                      pl.BlockSpec((None, d, d_e), lambda g, be, bn, ix: (be[g], 0, 0)),
                      pl.BlockSpec((None, d, d_e), lambda g, be, bn, ix: (be[g], 0, 0)),
                      pl.BlockSpec((None, d_e, d), lambda g, be, bn, ix: (be[g], 0, 0))],
            out_specs=pl.BlockSpec(memory_space=pl.ANY),
            scratch_shapes=[pltpu.VMEM((BM, d), F32), pltpu.VMEM((BM, d), F32),
                            pltpu.VMEM((d, d_e), BF16), pltpu.VMEM((d, d_e), BF16), pltpu.VMEM((d_e, d), BF16),
                            pltpu.SemaphoreType.DMA, pltpu.SemaphoreType.DMA]),
        out_shape=jax.ShapeDtypeStruct((TOP_K * ntp, d), F32),
        compiler_params=_cparams(("arbitrary",)),
        name="experts",
    )(blk_e, blk_n, idx, h2, w_gate, w_up, w_down)


def _final_kernel(*refs, n_prompt_tiles, alpha, split):
    ys = refs[:TOP_K]
    rest = refs[TOP_K:]
    (w_ref, x1_ref, h2_ref, g2s, g2p, lng_ref, lnb_ref, wg_hbm, wu_hbm, wd_hbm) = rest[:10]
    if split:
        yp_ref, ysm_ref = rest[10:12]
        wg_bf, wu_bf, wd_bf, stage_a, stage_b, sem = rest[12:]
    else:
        x2_ref = rest[10]
        wg_bf, wu_bf, wd_bf, stage_a, stage_b, sem = rest[11:]
    i = pl.program_id(0)

    @pl.when(i == 0)
    def _():
        _load_weight_bf16(wg_hbm, wg_bf, stage_a, sem, stage_a.shape[1])
        _load_weight_bf16(wu_hbm, wu_bf, stage_a, sem, stage_a.shape[1])
        _load_weight_bf16(wd_hbm, wd_bf, stage_b, sem, stage_b.shape[1])

    w = w_ref[...]
    routed = w[:, 0:1] * ys[0][...]
    for k in range(1, TOP_K):
        routed = routed + w[:, k:k + 1] * ys[k][...]
    hb = h2_ref[...].astype(BF16)
    sg = jnp.dot(hb, wg_bf[...], preferred_element_type=F32)
    su = jnp.dot(hb, wu_bf[...], preferred_element_type=F32)
    shared = jnp.dot((_silu(sg) * su).astype(BF16), wd_bf[...], preferred_element_type=F32)
    is_s = i >= n_prompt_tiles
    g2 = jnp.where(is_s, g2s[...], g2p[...])
    x2 = _layer_norm(alpha * x1_ref[...] + g2 * (routed + shared), lng_ref[...], lnb_ref[...])
    if split:
        @pl.when(i < n_prompt_tiles)
        def _():
            yp_ref[...] = x2

        @pl.when(i == n_prompt_tiles)
        def _():
            ysm_ref[...] = x2
    else:
        x2_ref[...] = x2


def _final(yslots, wsel, x1, h2, mod_s, mod_p, l, ln_g, ln_b, ws_gate, ws_up, ws_down,
           n_prompt_rows, rows_per_batch, nb, alpha, split):
    ntp, d = x1.shape
    tm = TM_FINAL
    d_s = ws_gate.shape[1]
    npt = n_prompt_rows // tm
    tpb = rows_per_batch // tm
    nt = ntp // tm
    tile = pl.BlockSpec((tm, d), lambda i: (i, 0))
    specs = [pl.BlockSpec((tm, d), functools.partial(lambda i, k: (k * nt + i, 0), k=k)) for k in range(TOP_K)]
    specs += [pl.BlockSpec((tm, TOP_K), lambda i: (i, 0)), tile, tile]
    specs += _mod_specs(l, 5, d, tm, npt, tpb, nb)
    specs += [pl.BlockSpec((1, d), lambda i: (0, 0))] * 2
    specs += [pl.BlockSpec(memory_space=pl.ANY)] * 3
    if split:
        out_specs = [pl.BlockSpec((tm, d), lambda i: (jnp.minimum(i, npt - 1), 0)),
                     pl.BlockSpec((tm, d), lambda i: (0, 0))]
        out_shape = [jax.ShapeDtypeStruct((n_prompt_rows, d), F32), jax.ShapeDtypeStruct((tm, d), F32)]
    else:
        out_specs = tile
        out_shape = jax.ShapeDtypeStruct((ntp, d), F32)
    ck = 256
    return pl.pallas_call(
        functools.partial(_final_kernel, n_prompt_tiles=npt, alpha=alpha, split=split),
        grid=(nt,),
        in_specs=specs, out_specs=out_specs, out_shape=out_shape,
        scratch_shapes=[pltpu.VMEM((d, d_s), BF16), pltpu.VMEM((d, d_s), BF16), pltpu.VMEM((d_s, d), BF16),
                        pltpu.VMEM((2, ck, d_s), F32), pltpu.VMEM((2, ck, d), F32), pltpu.SemaphoreType.DMA((2,))],
        compiler_params=_cparams(("arbitrary",)),
        name="moe_final",
    )(*([yslots] * TOP_K), wsel, x1, h2, mod_s, mod_p, ln_g.reshape(1, d), ln_b.reshape(1, d),
      ws_gate, ws_up, ws_down)


def kernel(x_prompt, x_sample, state_hgrn, state_pool, c_prompt, c_sample, mix0_w_in, mix0_w_out, hgrn_lb_table, hgrn_rms_w, pool_w, pool_scale, mix1_w_in, gmlp_ln_g, gmlp_ln_b, gmlp_w_sp, gmlp_b_sp, mix1_w_out, ada_w, ada_b, ln_mix_g, ln_mix_b, ln_ffn_g, ln_ffn_b, moe_w_router, moe_router_bias, moe_w_gate, moe_w_up, moe_w_down, shared_w_gate, shared_w_up, shared_w_down):
    nb, t, d = x_prompt.shape
    ns = x_sample.shape[0]
    depth = ada_w.shape[0]
    n_heads, dk = state_hgrn.shape[2], state_hgrn.shape[3]
    d_a = n_heads * dk
    d_b = state_pool.shape[3]
    n_hist = state_pool.shape[2]
    hist_pad = 16
    n_e = moe_w_router.shape[2]
    alpha = (2.0 * depth) ** 0.25
    n_p = nb * t
    npt = n_p // TM
    tpb = t // TM
    assert t % 512 == 0 and ns <= TM and ns % 8 == 0 and n_hist < hist_pad and d_a % d_b == 0

    x_all = jnp.concatenate([x_prompt.reshape(n_p, d), x_sample.reshape(ns, d), jnp.zeros((TM - ns, d), F32)], axis=0)
    c_all = jnp.concatenate([c_sample, jnp.zeros((TM - ns, d), F32), c_prompt, jnp.zeros((8 - nb % 8, d), F32)], axis=0)
    mod_s, mod_p = _adaln(c_all, ada_w, ada_b, TM)
    mod_p = mod_p.reshape(depth, mod_p.shape[1], 1, N_MOD * d)

    hgrn_p, pool_p, hgrn_s, pool_s, chunk_v = [], [], [], [], []
    x = x_all
    for l in range(depth):
        if l % 2 == 0:
            e = l // 2
            z = _inproj(x, mod_s, mod_p, l, mix0_w_in[e], npt, tpb, nb)
            o_a_p, s_p = _gla_prompt(z, hgrn_lb_table, hgrn_rms_w[e], l, nb, t, n_heads, dk)
            o_b_p, tail = _pool_prompt(z, pool_w[e], pool_scale[e], nb, t, d_b, 4 * d_a // d_b, hist_pad)
            hgrn_p.append(s_p)
            pool_p.append(tail[:, hist_pad - n_hist:])
            zs = z[n_p:n_p + ns]
            qt = zs[:, 0:d_a].reshape(ns, n_heads, dk).transpose(1, 2, 0)
            ft = zs[:, d_a:2 * d_a].reshape(ns, n_heads, dk).transpose(1, 2, 0)
            lbt_col = hgrn_lb_table.reshape(-1, n_heads, dk, 1)
            o_a_s, s_s = _gla_sample(qt, ft, lbt_col, zs[:, 2 * d_a:3 * d_a], zs[:, 3 * d_a:4 * d_a],
                                     hgrn_rms_w[e], state_hgrn[e], l)
            o_b_s, buf_s = _pool_sample(state_pool[e].reshape(ns, n_hist * d_b), z, pool_w[e], pool_scale[e],
                                        ns, d_b, n_p // ns, 4 * d_a // d_b, n_hist)
            hgrn_s.append(s_s)
            pool_s.append(buf_s.reshape(ns, n_hist, d_b))
            pad = jnp.zeros((TM - ns, d_a), BF16)
            o_a = jnp.concatenate([o_a_p, o_a_s.astype(BF16), pad], axis=0)
            o_b = jnp.concatenate([o_b_p, o_b_s, jnp.zeros((TM - ns, d_b), BF16)], axis=0)
            lhs, w_out = [o_a, o_b], mix0_w_out[e]
        else:
            o = l // 2
            u, v = _inproj(x, mod_s, mod_p, l, mix1_w_in[o], npt, tpb, nb, ln=(gmlp_ln_g[o], gmlp_ln_b[o]))
            chunk_v.append(v[n_p:n_p + ns].reshape(ns, 1, -1))
            lhs, w_out = [_spatial(u, v, gmlp_w_sp[o], gmlp_b_sp[o], npt)], mix1_w_out[o]
        x1, h2, logits_t = _outproj(lhs, x, mod_s, mod_p, l, w_out, ln_mix_g[l], ln_mix_b[l],
                                    moe_w_router[l].T, npt, tpb, nb, alpha)
        eidx_t, wsel_t = _route(logits_t, moe_router_bias[l])
        blk_e, blk_n, idx = _dispatch_tables(eidx_t, n_e)
        yslots = _experts(h2, blk_e, blk_n, idx, moe_w_gate[l], moe_w_up[l], moe_w_down[l])
        last = l == depth - 1
        res = _final(yslots, wsel_t.T, x1, h2, mod_s, mod_p, l, ln_ffn_g[l], ln_ffn_b[l],
                     shared_w_gate[l], shared_w_up[l], shared_w_down[l], n_p, t, nb, alpha, split=last)
        if last:
            y_p, y_s = res
        else:
            x = res
    return (y_p.reshape(nb, t, d), y_s[:ns].reshape(ns, 1, d),
            jnp.stack(hgrn_p), jnp.stack(pool_p), jnp.stack(hgrn_s), jnp.stack(pool_s), jnp.stack(chunk_v))
```

```python
import functools

import jax
import jax.numpy as jnp
from jax import lax
from jax.experimental import pallas as pl
from jax.experimental.pallas import tpu as pltpu

F32 = jnp.float32
BF16 = jnp.bfloat16

PAST_LEN = 16384
GLA_CHUNK = 32
POOL_WINDOWS = (2, 4, 8, 16)
C_CHUNK = 128
TOP_K = 8
N_GROUPS = 8
TOPK_GROUPS = 4
ROUTED_SCALE = 2.5
LN_EPS = 1e-5
RMS_EPS = 1e-6
N_MOD = 6

TM = 256
TM_FINAL = 128
BM = 256
ROW_CHUNK = 32
VMEM_LIMIT = 56 * 1024 * 1024


def _cparams(sem):
    return pltpu.CompilerParams(dimension_semantics=sem, vmem_limit_bytes=VMEM_LIMIT)


def _silu(x):
    return x * jax.nn.sigmoid(x)


def _layer_norm(y, g, b):
    mu = jnp.mean(y, axis=-1, keepdims=True)
    d = y - mu
    var = jnp.mean(d * d, axis=-1, keepdims=True)
    return d * lax.rsqrt(var + LN_EPS) * g + b


def _load_weight_bf16(w_hbm, w_bf, stage, sem, ck):
    n = w_hbm.shape[0] // ck

    def cp(c, slot):
        return pltpu.make_async_copy(w_hbm.at[pl.ds(pl.multiple_of(c * ck, ck), ck)], stage.at[slot], sem.at[slot])

    cp(0, 0).start()

    def body(c, carry):
        slot = c % 2

        @pl.when(c + 1 < n)
        def _():
            cp(c + 1, 1 - slot).start()

        cp(c, slot).wait()
        w_bf[pl.ds(pl.multiple_of(c * ck, ck), ck), :] = stage[slot].astype(BF16)
        return carry

    lax.fori_loop(0, n, body, 0)


def _mod_specs(l, j, d, tm, n_prompt_tiles, tiles_per_batch, nb):
    s = pl.BlockSpec((None, tm, d), lambda i: (l, jnp.maximum(i - n_prompt_tiles, 0), j))
    p = pl.BlockSpec((None, None, 1, d), lambda i: (l, jnp.minimum(i // tiles_per_batch, nb - 1), 0, j))
    return [s, p]


def _adaln_kernel(c_ref, w_ref, b_ref, os_ref, op_ref, *, ns_rows):
    c = c_ref[...]
    s = _silu(c).astype(BF16)
    r = jnp.dot(s, w_ref[...].astype(BF16), preferred_element_type=F32) + b_ref[...]
    os_ref[...] = r[:ns_rows]
    op_ref[...] = r[ns_rows:]


def _adaln(c_all, ada_w, ada_b, ns_rows):
    depth, d, n6 = ada_w.shape
    cm = c_all.shape[0]
    tn = 1024
    return pl.pallas_call(
        functools.partial(_adaln_kernel, ns_rows=ns_rows),
        grid=(depth, n6 // tn),
        in_specs=[pl.BlockSpec((cm, d), lambda l, j: (0, 0)),
                  pl.BlockSpec((None, d, tn), lambda l, j: (l, 0, j)),
                  pl.BlockSpec((None, 1, tn), lambda l, j: (l, 0, j))],
        out_specs=[pl.BlockSpec((None, ns_rows, tn), lambda l, j: (l, 0, j)),
                   pl.BlockSpec((None, cm - ns_rows, tn), lambda l, j: (l, 0, j))],
        out_shape=[jax.ShapeDtypeStruct((depth, ns_rows, n6), F32),
                   jax.ShapeDtypeStruct((depth, cm - ns_rows, n6), F32)],
        compiler_params=_cparams(("arbitrary", "arbitrary")),
        name="adaln",
    )(c_all, ada_w, ada_b.reshape(depth, 1, n6))


def _inproj_kernel(x_ref, scs_ref, scp_ref, shs_ref, shp_ref, w_hbm, *rest, n_prompt_tiles, nc, odd):
    if odd:
        lng_ref, lnb_ref, u_ref, v_ref, w_bf, hb, stage, sem = rest
    else:
        z_ref, w_bf, hb, stage, sem = rest
    i = pl.program_id(0)

    @pl.when(i == 0)
    def _():
        _load_weight_bf16(w_hbm, w_bf, stage, sem, stage.shape[1])

    is_s = i >= n_prompt_tiles
    for r0 in range(0, hb.shape[0], ROW_CHUNK):
        rs = slice(r0, r0 + ROW_CHUNK)
        sc = jnp.where(is_s, scs_ref[rs, :], scp_ref[...])
        sh = jnp.where(is_s, shs_ref[rs, :], shp_ref[...])
        hb[rs, :] = (x_ref[rs, :] * (1.0 + sc) + sh).astype(BF16)
    n = w_bf.shape[1]
    for c in range(n // nc):
        z = jnp.dot(hb[...], w_bf[:, c * nc:(c + 1) * nc], preferred_element_type=F32)
        if not odd:
            z_ref[:, c * nc:(c + 1) * nc] = z
        else:
            z = jax.nn.gelu(z)
            half = n // 2
            if c * nc < half:
                u_ref[:, c * nc:(c + 1) * nc] = z
            else:
                v_ref[:, c * nc - half:(c + 1) * nc - half] = z
    if odd:
        v_ref[...] = _layer_norm(v_ref[...], lng_ref[...], lnb_ref[...])


def _inproj(x_all, mod_s, mod_p, l, w, n_prompt_tiles, tiles_per_batch, nb, ln=None):
    ntp, d = x_all.shape
    n = w.shape[1]
    odd = ln is not None
    ck, nc = 128, 1024
    specs = [pl.BlockSpec((TM, d), lambda i: (i, 0))]
    specs += _mod_specs(l, 1, d, TM, n_prompt_tiles, tiles_per_batch, nb)
    specs += _mod_specs(l, 0, d, TM, n_prompt_tiles, tiles_per_batch, nb)
    specs += [pl.BlockSpec(memory_space=pl.ANY)]
    args = [x_all, mod_s, mod_p, mod_s, mod_p, w]
    if odd:
        specs += [pl.BlockSpec((1, n // 2), lambda i: (0, 0))] * 2
        args += [ln[0].reshape(1, -1), ln[1].reshape(1, -1)]
        out_specs = [pl.BlockSpec((TM, n // 2), lambda i: (i, 0))] * 2
        out_shape = [jax.ShapeDtypeStruct((ntp, n // 2), F32)] * 2
    else:
        out_specs = pl.BlockSpec((TM, n), lambda i: (i, 0))
        out_shape = jax.ShapeDtypeStruct((ntp, n), F32)
    return pl.pallas_call(
        functools.partial(_inproj_kernel, n_prompt_tiles=n_prompt_tiles, nc=nc, odd=odd),
        grid=(ntp // TM,),
        in_specs=specs, out_specs=out_specs, out_shape=out_shape,
        scratch_shapes=[pltpu.VMEM((d, n), BF16), pltpu.VMEM((TM, d), BF16),
                        pltpu.VMEM((2, ck, n), F32), pltpu.SemaphoreType.DMA((2,))],
        compiler_params=_cparams(("arbitrary",)),
        name="inproj_odd" if odd else "inproj_even",
    )(*args)


def _forget_lower_bound(rows, l):
    m = functools.reduce(jnp.maximum, rows)
    e = [jnp.exp(r - m) for r in rows]
    return sum(e[:l + 1]) / sum(e)


def _gla_prompt_kernel(q_ref, f_ref, v_ref, g_ref, lbt_ref, rms_ref, o_ref, s_ref,
                       st, qe_s, ke_s, kd_s, es_s, o_s, *, l, n_heads, dk):
    tb = pl.program_id(1)
    rows = q_ref.shape[0]
    nchunk = rows // GLA_CHUNK

    @pl.when(tb == 0)
    def _():
        st[...] = jnp.zeros_like(st)

    lb = _forget_lower_bound([lbt_ref[r:r + 1, :] for r in range(lbt_ref.shape[0])], l)
    fp = f_ref[...]
    logf = jnp.log(lb + (1.0 - lb) * jax.nn.sigmoid(fp))
    k = (1.0 - lb) * jax.nn.sigmoid(-fp)
    rid = lax.broadcasted_iota(jnp.int32, logf.shape, 0) % GLA_CHUNK
    bcum = logf
    s = 1
    while s < GLA_CHUNK:
        bcum = bcum + jnp.where(rid >= s, pltpu.roll(bcum, s, axis=0), 0.0)
        s *= 2
    b3 = bcum.reshape(nchunk, GLA_CHUNK, bcum.shape[1])
    blast = b3[:, GLA_CHUNK - 1:GLA_CHUNK, :]
    es_s[...] = jnp.exp(blast).reshape(nchunk, bcum.shape[1])
    blast_full = jnp.broadcast_to(blast, b3.shape).reshape(bcum.shape)
    qe_s[...] = q_ref[...] * jnp.exp(bcum)
    ke_s[...] = k * jnp.exp(-bcum)
    kd_s[...] = k * jnp.exp(blast_full - bcum)

    ri = lax.broadcasted_iota(jnp.int32, (GLA_CHUNK, GLA_CHUNK), 0)
    ci = lax.broadcasted_iota(jnp.int32, (GLA_CHUNK, GLA_CHUNK), 1)
    causal = ri >= ci

    def chunk_step(c, carry):
        r0 = pl.multiple_of(c * GLA_CHUNK, GLA_CHUNK)
        es_row = es_s[pl.ds(c, 1), :]
        for h in range(n_heads):
            cols = slice(h * dk, (h + 1) * dk)
            qe = qe_s[pl.ds(r0, GLA_CHUNK), cols]
            ke = ke_s[pl.ds(r0, GLA_CHUNK), cols]
            kd = kd_s[pl.ds(r0, GLA_CHUNK), cols]
            v = v_ref[pl.ds(r0, GLA_CHUNK), cols]
            att = lax.dot_general(qe, ke, (((1,), (1,)), ((), ())), preferred_element_type=F32)
            att = jnp.where(causal, att, 0.0)
            sth = st[h]
            o = jnp.dot(att, v, preferred_element_type=F32)
            o = o + lax.dot_general(qe, sth, (((1,), (1,)), ((), ())), preferred_element_type=F32)
            upd = lax.dot_general(v, kd, (((0,), (0,)), ((), ())), preferred_element_type=F32)
            st[h] = es_row[:, cols] * sth + upd
            o_s[pl.ds(r0, GLA_CHUNK), cols] = o
        return carry

    lax.fori_loop(0, nchunk, chunk_step, 0)

    for h in range(n_heads):
        cols = slice(h * dk, (h + 1) * dk)
        o = o_s[:, cols]
        o = o * lax.rsqrt(jnp.mean(o * o, axis=-1, keepdims=True) + RMS_EPS)
        o_ref[:, cols] = (o * rms_ref[:, cols] * _silu(g_ref[:, cols])).astype(o_ref.dtype)

    @pl.when(tb == pl.num_programs(1) - 1)
    def _():
        for h in range(n_heads):
            s_ref[0, h] = st[h].T


def _gla_prompt(z, lb_table, rms_w, l, nb, t, n_heads, dk):
    d_a = n_heads * dk
    tbk = 512
    nt = t // tbk
    row = lambda b, tb: b * nt + tb
    zspec = lambda sec: pl.BlockSpec((tbk, d_a), lambda b, tb: (row(b, tb), sec))
    return pl.pallas_call(
        functools.partial(_gla_prompt_kernel, l=l, n_heads=n_heads, dk=dk),
        grid=(nb, nt),
        in_specs=[zspec(0), zspec(1), zspec(2), zspec(3),
                  pl.BlockSpec(lb_table.shape, lambda b, tb: (0, 0)),
                  pl.BlockSpec((1, d_a), lambda b, tb: (0, 0))],
        out_specs=[pl.BlockSpec((tbk, d_a), lambda b, tb: (row(b, tb), 0)),
                   pl.BlockSpec((1, n_heads, dk, dk), lambda b, tb: (b, 0, 0, 0))],
        out_shape=[jax.ShapeDtypeStruct((nb * t, d_a), BF16),
                   jax.ShapeDtypeStruct((nb, n_heads, dk, dk), F32)],
        scratch_shapes=[pltpu.VMEM((n_heads, dk, dk), F32),
                        pltpu.VMEM((tbk, d_a), F32), pltpu.VMEM((tbk, d_a), F32), pltpu.VMEM((tbk, d_a), F32),
                        pltpu.VMEM((tbk // GLA_CHUNK, d_a), F32), pltpu.VMEM((tbk, d_a), F32)],
        compiler_params=_cparams(("arbitrary", "arbitrary")),
        name="gla_prompt",
    )(z, z, z, z, lb_table, rms_w.reshape(1, d_a))


def _gla_sample_kernel(qt_ref, ft_ref, lbt_ref, v_ref, g_ref, rms_ref, s0_ref, o_ref, s_ref, *, l):
    ns = s0_ref.shape[0]
    lbc = _forget_lower_bound([lbt_ref[r] for r in range(lbt_ref.shape[0])], l)
    fp = ft_ref[...]
    fcol = lbc + (1.0 - lbc) * jax.nn.sigmoid(fp)
    kcol = (1.0 - lbc) * jax.nn.sigmoid(-fp)
    qcol = qt_ref[...]
    for b in range(ns):
        v = v_ref[b:b + 1, :]
        s_new = fcol[:, b:b + 1] * s0_ref[b, 0] + kcol[:, b:b + 1] * v
        s_ref[b, 0] = s_new
        o_ref[b:b + 1, :] = jnp.sum(qcol[:, b:b + 1] * s_new, axis=0, keepdims=True)
    o = o_ref[...]
    o = o * lax.rsqrt(jnp.mean(o * o, axis=-1, keepdims=True) + RMS_EPS)
    o_ref[...] = o * rms_ref[...] * _silu(g_ref[...])


def _gla_sample(qt, ft, lbt_col, v_s, g_s, rms_w, s0, l):
    ns, n_heads, dk, dv = s0.shape
    return pl.pallas_call(
        functools.partial(_gla_sample_kernel, l=l),
        grid=(n_heads,),
        in_specs=[pl.BlockSpec((None, dk, ns), lambda h: (h, 0, 0)),
                  pl.BlockSpec((None, dk, ns), lambda h: (h, 0, 0)),
                  pl.BlockSpec((lbt_col.shape[0], None, dk, 1), lambda h: (0, h, 0, 0)),
                  pl.BlockSpec((ns, dv), lambda h: (0, h)),
                  pl.BlockSpec((ns, dv), lambda h: (0, h)),
                  pl.BlockSpec((1, dv), lambda h: (0, h)),
                  pl.BlockSpec((ns, 1, dk, dv), lambda h: (0, h, 0, 0))],
        out_specs=[pl.BlockSpec((ns, dv), lambda h: (0, h)),
                   pl.BlockSpec((ns, 1, dk, dv), lambda h: (0, h, 0, 0))],
        out_shape=[jax.ShapeDtypeStruct((ns, n_heads * dv), F32),
                   jax.ShapeDtypeStruct(s0.shape, F32)],
        compiler_params=_cparams(("arbitrary",)),
        name="gla_sample",
    )(qt, ft, lbt_col, v_s, g_s, rms_w.reshape(1, -1), s0)


def _pool_mix(win_sums, u, inv_cnt, pw_ref, ps_ref, o_ref, gb):
    for gi in range(len(POOL_WINDOWS)):
        cols = slice(gi * gb, (gi + 1) * gb)
        pooled = win_sums[gi] * inv_cnt[gi] - u[:, cols]
        mixed = jnp.dot(pooled.astype(BF16), pw_ref[gi].astype(BF16), preferred_element_type=F32)
        o_ref[:, cols] = (mixed * ps_ref[:, cols]).astype(o_ref.dtype)


def _pool_prompt_kernel(u_ref, pw_ref, ps_ref, o_ref, tail_ref, buf, *, hist_pad):
    tb = pl.program_id(1)
    rows, d_b = u_ref.shape
    gb = d_b // len(POOL_WINDOWS)

    @pl.when(tb == 0)
    def _():
        buf[0:hist_pad, :] = jnp.zeros((hist_pad, d_b), F32)

    u = u_ref[...]
    buf[hist_pad:, :] = u
    acc = buf[...]
    sums = []
    span = 1
    for gi, w in enumerate(POOL_WINDOWS):
        while span < w:
            acc = acc + pltpu.roll(acc, span, axis=0)
            span *= 2
        sums.append(acc[hist_pad:, gi * gb:(gi + 1) * gb])
    pos = tb * rows + lax.broadcasted_iota(jnp.int32, (rows, gb), 0)
    inv_cnt = [1.0 / jnp.minimum(pos + 1, w).astype(F32) for w in POOL_WINDOWS]
    _pool_mix(sums, u, inv_cnt, pw_ref, ps_ref, o_ref, gb)
    buf[0:hist_pad, :] = u[rows - hist_pad:, :]
    tail_ref[0] = u[rows - hist_pad:, :]


def _pool_prompt(z, pool_w, pool_scale, nb, t, d_b, u_col_block, hist_pad):
    tbk = 512
    nt = t // tbk
    return pl.pallas_call(
        functools.partial(_pool_prompt_kernel, hist_pad=hist_pad),
        grid=(nb, nt),
        in_specs=[pl.BlockSpec((tbk, d_b), lambda b, tb: (b * nt + tb, u_col_block)),
                  pl.BlockSpec(pool_w.shape, lambda b, tb: (0, 0, 0)),
                  pl.BlockSpec((1, d_b), lambda b, tb: (0, 0))],
        out_specs=[pl.BlockSpec((tbk, d_b), lambda b, tb: (b * nt + tb, 0)),
                   pl.BlockSpec((1, hist_pad, d_b), lambda b, tb: (b, 0, 0))],
        out_shape=[jax.ShapeDtypeStruct((nb * t, d_b), BF16),
                   jax.ShapeDtypeStruct((nb, hist_pad, d_b), F32)],
        scratch_shapes=[pltpu.VMEM((hist_pad + tbk, d_b), F32)],
        compiler_params=_cparams(("arbitrary", "arbitrary")),
        name="pool_prompt",
    )(z, pool_w, pool_scale.reshape(1, d_b))


def _pool_sample_kernel(hist_ref, u_ref, pw_ref, ps_ref, o_ref, nb_ref, *, n_hist):
    ns, d_b = u_ref.shape
    gb = d_b // len(POOL_WINDOWS)
    u = u_ref[...]
    row = lambda j: hist_ref[:, j * d_b:(j + 1) * d_b]
    sums, inv_cnt = [], []
    for gi, w in enumerate(POOL_WINDOWS):
        cols = slice(gi * gb, (gi + 1) * gb)
        acc = u[:, cols]
        for j in range(n_hist - (w - 1), n_hist):
            acc = acc + row(j)[:, cols]
        sums.append(acc)
        inv_cnt.append(1.0 / float(min(PAST_LEN + 1, w)))
    _pool_mix(sums, u, inv_cnt, pw_ref, ps_ref, o_ref, gb)
    nb_ref[:, 0:(n_hist - 1) * d_b] = hist_ref[:, d_b:]
    nb_ref[:, (n_hist - 1) * d_b:] = u


def _pool_sample(hist2d, z, pool_w, pool_scale, ns, d_b, u_row_block, u_col_block, n_hist):
    return pl.pallas_call(
        functools.partial(_pool_sample_kernel, n_hist=n_hist),
        grid=(1,),
        in_specs=[pl.BlockSpec(hist2d.shape, lambda i: (0, 0)),
                  pl.BlockSpec((ns, d_b), lambda i: (u_row_block, u_col_block)),
                  pl.BlockSpec(pool_w.shape, lambda i: (0, 0, 0)),
                  pl.BlockSpec((1, d_b), lambda i: (0, 0))],
        out_specs=[pl.BlockSpec((ns, d_b), lambda i: (0, 0)),
                   pl.BlockSpec(hist2d.shape, lambda i: (0, 0))],
        out_shape=[jax.ShapeDtypeStruct((ns, d_b), BF16),
                   jax.ShapeDtypeStruct(hist2d.shape, F32)],
        compiler_params=_cparams(("arbitrary",)),
        name="pool_sample",
    )(hist2d, z, pool_w, pool_scale.reshape(1, d_b))


def _spatial_kernel(u_ref, v_ref, wsp_ref, bt_ref, w00_ref, b0_ref, o_ref, *, n_prompt_tiles, n_heads):
    i = pl.program_id(0)
    rows, d_c = u_ref.shape
    gc = d_c // n_heads

    @pl.when(i < n_prompt_tiles)
    def _():
        ri = lax.broadcasted_iota(jnp.int32, (C_CHUNK, C_CHUNK), 0)
        ci = lax.broadcasted_iota(jnp.int32, (C_CHUNK, C_CHUNK), 1)
        for h in range(n_heads):
            wc = jnp.where(ri >= ci, wsp_ref[h], 0.0).astype(BF16)
            bias = bt_ref[:, h:h + 1]
            for c in range(rows // C_CHUNK):
                rs = slice(c * C_CHUNK, (c + 1) * C_CHUNK)
                cs = slice(h * gc, (h + 1) * gc)
                mixed = jnp.dot(wc, v_ref[rs, cs].astype(BF16), preferred_element_type=F32) + bias
                o_ref[rs, cs] = (u_ref[rs, cs] * mixed).astype(o_ref.dtype)

    @pl.when(i >= n_prompt_tiles)
    def _():
        o_ref[...] = (u_ref[...] * (v_ref[...] * w00_ref[...] + b0_ref[...])).astype(o_ref.dtype)


def _spatial(u, v, w_sp, b_sp, n_prompt_tiles):
    ntp, d_c = u.shape
    n_heads = w_sp.shape[0]
    gc = d_c // n_heads
    w00 = jnp.repeat(w_sp[:, 0, 0], gc).reshape(1, d_c)
    b0 = jnp.repeat(b_sp[:, 0], gc).reshape(1, d_c)
    tile = pl.BlockSpec((TM, d_c), lambda i: (i, 0))
    return pl.pallas_call(
        functools.partial(_spatial_kernel, n_prompt_tiles=n_prompt_tiles, n_heads=n_heads),
        grid=(ntp // TM,),
        in_specs=[tile, tile,
                  pl.BlockSpec(w_sp.shape, lambda i: (0, 0, 0)),
                  pl.BlockSpec((C_CHUNK, n_heads), lambda i: (0, 0)),
                  pl.BlockSpec((1, d_c), lambda i: (0, 0)),
                  pl.BlockSpec((1, d_c), lambda i: (0, 0))],
        out_specs=tile,
        out_shape=jax.ShapeDtypeStruct((ntp, d_c), BF16),
        compiler_params=_cparams(("arbitrary",)),
        name="spatial",
    )(u, v, w_sp, b_sp.T, w00, b0)


def _outproj_kernel(*refs, n_lhs, n_prompt_tiles, alpha):
    lhs = refs[:n_lhs]
    (x_ref, g1s, g1p, sc2s, sc2p, sh2s, sh2p, lng_ref, lnb_ref, wr_ref, w_hbm,
     x1_ref, h2_ref, lg_ref, w_bf, stage, sem) = refs[n_lhs:]
    i = pl.program_id(0)

    @pl.when(i == 0)
    def _():
        _load_weight_bf16(w_hbm, w_bf, stage, sem, stage.shape[1])

    is_s = i >= n_prompt_tiles
    out = None
    k0 = 0
    for a_ref in lhs:
        kk = a_ref.shape[1]
        part = jnp.dot(a_ref[...], w_bf[k0:k0 + kk, :], preferred_element_type=F32)
        out = part if out is None else out + part
        k0 += kk
    for r0 in range(0, out.shape[0], ROW_CHUNK):
        rs = slice(r0, r0 + ROW_CHUNK)
        g1 = jnp.where(is_s, g1s[rs, :], g1p[...])
        x1 = _layer_norm(alpha * x_ref[rs, :] + g1 * out[rs], lng_ref[...], lnb_ref[...])
        x1_ref[rs, :] = x1
        sc2 = jnp.where(is_s, sc2s[rs, :], sc2p[...])
        sh2 = jnp.where(is_s, sh2s[rs, :], sh2p[...])
        h2_ref[rs, :] = x1 * (1.0 + sc2) + sh2
    lg_ref[...] = lax.dot_general(wr_ref[...], h2_ref[...], (((1,), (1,)), ((), ())),
                                  precision=lax.Precision.HIGHEST, preferred_element_type=F32)


def _outproj(lhs, x_all, mod_s, mod_p, l, w_out, ln_g, ln_b, w_router_t, n_prompt_tiles, tiles_per_batch, nb, alpha):
    ntp, d = x_all.shape
    n_e = w_router_t.shape[0]
    ck = 256
    tile = pl.BlockSpec((TM, d), lambda i: (i, 0))
    specs = [pl.BlockSpec((TM, a.shape[1]), lambda i: (i, 0)) for a in lhs] + [tile]
    for j in (2, 4, 3):
        specs += _mod_specs(l, j, d, TM, n_prompt_tiles, tiles_per_batch, nb)
    specs += [pl.BlockSpec((1, d), lambda i: (0, 0))] * 2
    specs += [pl.BlockSpec((n_e, d), lambda i: (0, 0)), pl.BlockSpec(memory_space=pl.ANY)]
    return pl.pallas_call(
        functools.partial(_outproj_kernel, n_lhs=len(lhs), n_prompt_tiles=n_prompt_tiles, alpha=alpha),
        grid=(ntp // TM,),
        in_specs=specs,
        out_specs=[tile, tile, pl.BlockSpec((n_e, TM), lambda i: (0, i))],
        out_shape=[jax.ShapeDtypeStruct((ntp, d), F32), jax.ShapeDtypeStruct((ntp, d), F32),
                   jax.ShapeDtypeStruct((n_e, ntp), F32)],
        scratch_shapes=[pltpu.VMEM(w_out.shape, BF16), pltpu.VMEM((2, ck, d), F32), pltpu.SemaphoreType.DMA((2,))],
        compiler_params=_cparams(("arbitrary",)),
        name="outproj",
    )(*lhs, x_all, *([mod_s, mod_p] * 3), ln_g.reshape(1, d), ln_b.reshape(1, d), w_router_t, w_out)


def _first_argmax(vals, iota, axis, size):
    m = jnp.max(vals, axis=axis, keepdims=True)
    idx = jnp.min(jnp.where(vals == m, iota, size), axis=axis, keepdims=True)
    return m, idx


def _route_kernel(lg_ref, bias_ref, e_ref, w_ref):
    n_e, tm = lg_ref.shape
    gsz = n_e // N_GROUPS
    neg = -jnp.inf
    scores = jax.nn.sigmoid(lg_ref[...])
    sel = scores + bias_ref[...]
    sel3 = sel.reshape(N_GROUPS, gsz, tm)
    io3 = lax.broadcasted_iota(jnp.int32, sel3.shape, 1)
    m1, i1 = _first_argmax(sel3, io3, 1, gsz)
    m2 = jnp.max(jnp.where(io3 == i1, neg, sel3), axis=1, keepdims=True)
    gscore = (m1 + m2).reshape(N_GROUPS, tm)
    iog = lax.broadcasted_iota(jnp.int32, gscore.shape, 0)
    gsel = jnp.zeros(gscore.shape, F32)
    cur = gscore
    for _ in range(TOPK_GROUPS):
        _, gi = _first_argmax(cur, iog, 0, N_GROUPS)
        pick = iog == gi
        gsel = jnp.where(pick, 1.0, gsel)
        cur = jnp.where(pick, neg, cur)
    esel = jnp.broadcast_to(gsel.reshape(N_GROUPS, 1, tm), sel3.shape).reshape(n_e, tm)
    ioe = lax.broadcasted_iota(jnp.int32, sel.shape, 0)
    iok = lax.broadcasted_iota(jnp.int32, (TOP_K, tm), 0)
    cur = jnp.where(esel > 0.5, sel, neg)
    eidx = jnp.zeros((TOP_K, tm), jnp.int32)
    sc = jnp.zeros((TOP_K, tm), F32)
    for r in range(TOP_K):
        _, ei = _first_argmax(cur, ioe, 0, n_e)
        pick = ioe == ei
        val = jnp.sum(jnp.where(pick, scores, 0.0), axis=0, keepdims=True)
        eidx = jnp.where(iok == r, ei, eidx)
        sc = jnp.where(iok == r, val, sc)
        cur = jnp.where(pick, neg, cur)
    e_ref[...] = eidx
    w_ref[...] = sc / jnp.sum(sc, axis=0, keepdims=True) * ROUTED_SCALE


def _route(logits_t, router_bias):
    n_e, ntp = logits_t.shape
    return pl.pallas_call(
        _route_kernel,
        grid=(ntp // TM,),
        in_specs=[pl.BlockSpec((n_e, TM), lambda i: (0, i)), pl.BlockSpec((n_e, 1), lambda i: (0, 0))],
        out_specs=[pl.BlockSpec((TOP_K, TM), lambda i: (0, i))] * 2,
        out_shape=[jax.ShapeDtypeStruct((TOP_K, ntp), jnp.int32), jax.ShapeDtypeStruct((TOP_K, ntp), F32)],
        compiler_params=_cparams(("arbitrary",)),
        name="route",
    )(logits_t, router_bias.reshape(n_e, 1))


def _dispatch_tables(eidx_t, n_e):
    flat_e = eidx_t.T.reshape(-1)
    nk = flat_e.shape[0]
    n_steps = nk // BM + n_e + 1
    abits = (nk + BM - 1).bit_length()
    experts = jnp.arange(n_e, dtype=jnp.int32)
    counts = jnp.sum((flat_e[:, None] == experts[None, :]).astype(jnp.int32), axis=0)
    nblk = (counts + BM - 1) // BM
    bends = jnp.cumsum(nblk)
    total = bends[-1]
    j = jnp.arange(BM, dtype=jnp.int32)
    real = jnp.left_shift(flat_e, abits) | jnp.arange(nk, dtype=jnp.int32)
    pad = jnp.where(j[None, :] < (nblk * BM - counts)[:, None],
                    jnp.left_shift(experts, abits)[:, None] | (nk + j)[None, :], jnp.iinfo(jnp.int32).max)
    ids = jnp.sort(jnp.concatenate([real, pad.reshape(-1)])) & ((1 << abits) - 1)
    table = jnp.concatenate([nk + j, ids, nk + j])
    g = jnp.arange(n_steps, dtype=jnp.int32)
    e_of = jnp.sum((bends[None, :] <= g[:, None]).astype(jnp.int32), axis=1)
    e_last = jnp.sum((bends <= total - 1).astype(jnp.int32))
    blk_e = jnp.minimum(jnp.where(g < total, e_of, e_last), n_e - 1)
    return blk_e.astype(jnp.int32), total.reshape(1).astype(jnp.int32), table


def _experts_kernel(blk_e, total_ref, table, h2_hbm, wg_ref, wu_ref, wd_ref, y_hbm,
                    xbuf, ybuf, xb, hb, wg_bf, wu_bf, wd_bf, gsem, ssem, *, ntp, nc):
    g = pl.program_id(0)
    total = total_ref[0]
    s = g % 2
    kbits = TOP_K.bit_length() - 1
    d = xbuf.shape[2]

    def gather(j, slot, r):
        a = table[(j + 1) * BM + r]
        tok = jnp.minimum(lax.shift_right_logical(a, kbits), ntp - 1)
        return pltpu.make_async_copy(h2_hbm.at[pl.ds(tok, 1)], xbuf.at[slot, pl.ds(r, 1)], gsem.at[slot])

    def scatter(j, slot, r):
        a = table[(j + 1) * BM + r]
        col = pl.multiple_of(jnp.bitwise_and(a, TOP_K - 1) * d, d)
        return pltpu.make_async_copy(ybuf.at[slot, pl.ds(r, 1)],
                                     y_hbm.at[pl.ds(lax.shift_right_logical(a, kbits), 1), pl.ds(col, d)],
                                     ssem.at[slot])

    def wait_gather(slot):
        for r in range(BM):
            pltpu.make_async_copy(h2_hbm.at[pl.ds(0, 1)], xbuf.at[slot, pl.ds(r, 1)], gsem.at[slot]).wait()

    def wait_scatter(slot):
        for r in range(BM):
            pltpu.make_async_copy(ybuf.at[slot, pl.ds(r, 1)], y_hbm.at[pl.ds(0, 1), pl.ds(0, d)], ssem.at[slot]).wait()

    @pl.when(g == 0)
    def _():
        ybuf[1] = jnp.zeros(ybuf.shape[1:], F32)

        def body(r, c):
            gather(0, 0, r).start()
            return c
        lax.fori_loop(0, BM, body, 0, unroll=8)

    @pl.when(g < total)
    def _():
        prev = blk_e[jnp.maximum(g - 1, 0)]

        @pl.when(jnp.logical_or(g == 0, blk_e[g] != prev))
        def _():
            wg_bf[...] = wg_ref[...].astype(BF16)
            wu_bf[...] = wu_ref[...].astype(BF16)
            wd_bf[...] = wd_ref[...].astype(BF16)

        wait_gather(s)

        @pl.when(g >= 1)
        def _():
            wait_scatter(s)

        d_e = wd_bf.shape[0]
        pieces = d_e // nc + d // nc
        rows = [list(range(p * BM // pieces, (p + 1) * BM // pieces)) for p in range(pieces)]

        def issue(p):
            for r in rows[p]:
                gather(g + 1, 1 - s, r).start()
                scatter(g - 1, 1 - s, r).start(priority=1)

        xb[...] = xbuf[s].astype(BF16)
        p = 0
        for c in range(d_e // nc):
            cols = slice(c * nc, (c + 1) * nc)
            hg = jnp.dot(xb[...], wg_bf[:, cols], preferred_element_type=F32)
            hu = jnp.dot(xb[...], wu_bf[:, cols], preferred_element_type=F32)
            hb[:, cols] = (_silu(hg) * hu).astype(BF16)
            issue(p)
            p += 1
        for c in range(d // nc):
            cols = slice(c * nc, (c + 1) * nc)
            ybuf[s, :, cols] = jnp.dot(hb[...], wd_bf[:, cols], preferred_element_type=F32)
            issue(p)
            p += 1

    @pl.when(g == total)
    def _():
        wait_gather(s)
        wait_scatter(s)

        def body(r, c):
            scatter(g - 1, 1 - s, r).start(priority=1)
            return c
        lax.fori_loop(0, BM, body, 0, unroll=8)
        wait_scatter(1 - s)


def _experts(h2, tables, w_gate, w_up, w_down, l):
    ntp, d = h2.shape
    d_e = w_gate.shape[-1]
    n_steps = tables[0].shape[0]
    wmap = lambda g, be, *_: (l, be[g], 0, 0)
    return pl.pallas_call(
        functools.partial(_experts_kernel, ntp=ntp, nc=256),
        grid_spec=pltpu.PrefetchScalarGridSpec(
            num_scalar_prefetch=3, grid=(n_steps,),
            in_specs=[pl.BlockSpec(memory_space=pl.ANY),
                      pl.BlockSpec((None, None, d, d_e), wmap),
                      pl.BlockSpec((None, None, d, d_e), wmap),
                      pl.BlockSpec((None, None, d_e, d), wmap)],
            out_specs=pl.BlockSpec(memory_space=pl.ANY),
            scratch_shapes=[pltpu.VMEM((2, BM, d), F32), pltpu.VMEM((2, BM, d), F32),
                            pltpu.VMEM((BM, d), BF16), pltpu.VMEM((BM, d_e), BF16),
                            pltpu.VMEM((d, d_e), BF16), pltpu.VMEM((d, d_e), BF16), pltpu.VMEM((d_e, d), BF16),
                            pltpu.SemaphoreType.DMA((2,)), pltpu.SemaphoreType.DMA((2,))]),
        out_shape=jax.ShapeDtypeStruct((ntp + BM // TOP_K, TOP_K * d), F32),
        compiler_params=_cparams(("arbitrary",)),
        name="experts",
    )(*tables, h2, w_gate, w_up, w_down)


def _final_kernel(*refs, l, n_prompt_tiles, alpha, split):
    ys_ref = refs[0]
    rest = refs[1:]
    (w_ref, x1_ref, h2_ref, g2s, g2p, lng_ref, lnb_ref, wg_hbm, wu_hbm, wd_hbm) = rest[:10]
    if split:
        yp_ref, ysm_ref = rest[10:12]
        wg_bf, wu_bf, wd_bf, stage_a, stage_b, sem, x2_s = rest[12:]
    else:
        x2_ref = rest[10]
        wg_bf, wu_bf, wd_bf, stage_a, stage_b, sem = rest[11:]
    i = pl.program_id(0)

    @pl.when(i == 0)
    def _():
        _load_weight_bf16(wg_hbm.at[l], wg_bf, stage_a, sem, stage_a.shape[1])
        _load_weight_bf16(wu_hbm.at[l], wu_bf, stage_a, sem, stage_a.shape[1])
        _load_weight_bf16(wd_hbm.at[l], wd_bf, stage_b, sem, stage_b.shape[1])

    hb = h2_ref[...].astype(BF16)
    sg = jnp.dot(hb, wg_bf[...], preferred_element_type=F32)
    su = jnp.dot(hb, wu_bf[...], preferred_element_type=F32)
    shared = jnp.dot((_silu(sg) * su).astype(BF16), wd_bf[...], preferred_element_type=F32)
    is_s = i >= n_prompt_tiles
    dst = x2_s if split else x2_ref
    d = x1_ref.shape[1]

    for r0 in range(0, shared.shape[0], ROW_CHUNK):
        rs = slice(r0, r0 + ROW_CHUNK)
        w = w_ref[rs, :]
        ffn = shared[rs]
        for k in range(TOP_K):
            ffn = ffn + w[:, k:k + 1] * ys_ref[rs, k * d:(k + 1) * d]
        g2 = jnp.where(is_s, g2s[rs, :], g2p[...])
        dst[rs, :] = _layer_norm(alpha * x1_ref[rs, :] + g2 * ffn, lng_ref[...], lnb_ref[...])

    if split:
        @pl.when(i < n_prompt_tiles)
        def _():
            yp_ref[...] = x2_s[...]

        @pl.when(i == n_prompt_tiles)
        def _():
            ysm_ref[...] = x2_s[...]


def _final(yslots, wsel, x1, h2, mod_s, mod_p, l, ln_g, ln_b, ws_gate, ws_up, ws_down,
           n_prompt_rows, rows_per_batch, nb, alpha, split):
    ntp, d = x1.shape
    tm = TM_FINAL
    d_s = ws_gate.shape[-1]
    npt = n_prompt_rows // tm
    tpb = rows_per_batch // tm
    nt = ntp // tm
    tile = pl.BlockSpec((tm, d), lambda i: (i, 0))
    specs = [pl.BlockSpec((tm, TOP_K * d), lambda i: (i, 0)), pl.BlockSpec((tm, TOP_K), lambda i: (i, 0)), tile, tile]
    specs += _mod_specs(l, 5, d, tm, npt, tpb, nb)
    specs += [pl.BlockSpec((1, d), lambda i: (0, 0))] * 2
    specs += [pl.BlockSpec(memory_space=pl.ANY)] * 3
    if split:
        out_specs = [pl.BlockSpec((tm, d), lambda i: (jnp.minimum(i, npt - 1), 0)),
                     pl.BlockSpec((tm, d), lambda i: (0, 0))]
        out_shape = [jax.ShapeDtypeStruct((n_prompt_rows, d), F32), jax.ShapeDtypeStruct((tm, d), F32)]
    else:
        out_specs = tile
        out_shape = jax.ShapeDtypeStruct((ntp, d), F32)
    ck = 256
    scratch = [pltpu.VMEM((d, d_s), BF16), pltpu.VMEM((d, d_s), BF16), pltpu.VMEM((d_s, d), BF16),
               pltpu.VMEM((2, ck, d_s), F32), pltpu.VMEM((2, ck, d), F32), pltpu.SemaphoreType.DMA((2,))]
    if split:
        scratch.append(pltpu.VMEM((tm, d), F32))
    return pl.pallas_call(
        functools.partial(_final_kernel, l=l, n_prompt_tiles=npt, alpha=alpha, split=split),
        grid=(nt,),
        in_specs=specs, out_specs=out_specs, out_shape=out_shape,
        scratch_shapes=scratch,
        compiler_params=_cparams(("arbitrary",)),
        name="moe_final",
    )(yslots, wsel, x1, h2, mod_s, mod_p, ln_g.reshape(1, d), ln_b.reshape(1, d),
      ws_gate, ws_up, ws_down)


def kernel(x_prompt, x_sample, state_hgrn, state_pool, c_prompt, c_sample, mix0_w_in, mix0_w_out, hgrn_lb_table, hgrn_rms_w, pool_w, pool_scale, mix1_w_in, gmlp_ln_g, gmlp_ln_b, gmlp_w_sp, gmlp_b_sp, mix1_w_out, ada_w, ada_b, ln_mix_g, ln_mix_b, ln_ffn_g, ln_ffn_b, moe_w_router, moe_router_bias, moe_w_gate, moe_w_up, moe_w_down, shared_w_gate, shared_w_up, shared_w_down):
    nb, t, d = x_prompt.shape
    ns = x_sample.shape[0]
    depth = ada_w.shape[0]
    n_heads, dk = state_hgrn.shape[2], state_hgrn.shape[3]
    d_a = n_heads * dk
    d_b = state_pool.shape[3]
    n_hist = state_pool.shape[2]
    hist_pad = 16
    n_e = moe_w_router.shape[2]
    alpha = (2.0 * depth) ** 0.25
    n_p = nb * t
    npt = n_p // TM
    tpb = t // TM
    assert t % 512 == 0 and ns <= TM and ns % 8 == 0 and n_hist < hist_pad and d_a % d_b == 0

    x_all = jnp.concatenate([x_prompt.reshape(n_p, d), x_sample.reshape(ns, d), jnp.zeros((TM - ns, d), F32)], axis=0)
    c_all = jnp.concatenate([c_sample, jnp.zeros((TM - ns, d), F32), c_prompt, jnp.zeros((8 - nb % 8, d), F32)], axis=0)
    mod_s, mod_p = _adaln(c_all, ada_w, ada_b, TM)
    mod_p = mod_p.reshape(depth, mod_p.shape[1], 1, N_MOD * d)

    hgrn_p, pool_p, hgrn_s, pool_s, chunk_v = [], [], [], [], []
    x = x_all
    for l in range(depth):
        if l % 2 == 0:
            e = l // 2
            z = _inproj(x, mod_s, mod_p, l, mix0_w_in[e], npt, tpb, nb)
            o_a_p, s_p = _gla_prompt(z, hgrn_lb_table, hgrn_rms_w[e], l, nb, t, n_heads, dk)
            o_b_p, tail = _pool_prompt(z, pool_w[e], pool_scale[e], nb, t, d_b, 4 * d_a // d_b, hist_pad)
            hgrn_p.append(s_p)
            pool_p.append(tail[:, hist_pad - n_hist:])
            zs = z[n_p:n_p + ns]
            qt = zs[:, 0:d_a].reshape(ns, n_heads, dk).transpose(1, 2, 0)
            ft = zs[:, d_a:2 * d_a].reshape(ns, n_heads, dk).transpose(1, 2, 0)
            lbt_col = hgrn_lb_table.reshape(-1, n_heads, dk, 1)
            o_a_s, s_s = _gla_sample(qt, ft, lbt_col, zs[:, 2 * d_a:3 * d_a], zs[:, 3 * d_a:4 * d_a],
                                     hgrn_rms_w[e], state_hgrn[e], l)
            o_b_s, buf_s = _pool_sample(state_pool[e].reshape(ns, n_hist * d_b), z, pool_w[e], pool_scale[e],
                                        ns, d_b, n_p // ns, 4 * d_a // d_b, n_hist)
            hgrn_s.append(s_s)
            pool_s.append(buf_s.reshape(ns, n_hist, d_b))
            pad = jnp.zeros((TM - ns, d_a), BF16)
            o_a = jnp.concatenate([o_a_p, o_a_s.astype(BF16), pad], axis=0)
            o_b = jnp.concatenate([o_b_p, o_b_s, jnp.zeros((TM - ns, d_b), BF16)], axis=0)
            lhs, w_out = [o_a, o_b], mix0_w_out[e]
        else:
            o = l // 2
            u, v = _inproj(x, mod_s, mod_p, l, mix1_w_in[o], npt, tpb, nb, ln=(gmlp_ln_g[o], gmlp_ln_b[o]))
            chunk_v.append(v[n_p:n_p + ns].reshape(ns, 1, -1))
            lhs, w_out = [_spatial(u, v, gmlp_w_sp[o], gmlp_b_sp[o], npt)], mix1_w_out[o]
        x1, h2, logits_t = _outproj(lhs, x, mod_s, mod_p, l, w_out, ln_mix_g[l], ln_mix_b[l],
                                    moe_w_router[l].T, npt, tpb, nb, alpha)
        eidx_t, wsel_t = _route(logits_t, moe_router_bias[l])
        yslots = _experts(h2, _dispatch_tables(eidx_t, n_e), moe_w_gate, moe_w_up, moe_w_down, l)
        last = l == depth - 1
        res = _final(yslots, wsel_t.T, x1, h2, mod_s, mod_p, l, ln_ffn_g[l], ln_ffn_b[l],
                     shared_w_gate, shared_w_up, shared_w_down, n_p, t, nb, alpha, split=last)
        if last:
            y_p, y_s = res
        else:
            x = res
    return (y_p.reshape(nb, t, d), y_s[:ns].reshape(ns, 1, d),
            jnp.stack(hgrn_p), jnp.stack(pool_p), jnp.stack(hgrn_s), jnp.stack(pool_s), jnp.stack(chunk_v))
```

```python
import functools

import jax
import jax.numpy as jnp
from jax import lax
from jax.experimental import pallas as pl
from jax.experimental.pallas import tpu as pltpu

F32 = jnp.float32
BF16 = jnp.bfloat16

PAST_LEN = 16384
GLA_CHUNK = 32
POOL_WINDOWS = (2, 4, 8, 16)
C_CHUNK = 128
TOP_K = 8
N_GROUPS = 8
TOPK_GROUPS = 4
ROUTED_SCALE = 2.5
LN_EPS = 1e-5
RMS_EPS = 1e-6
N_MOD = 6

TM = 256
TM_FINAL = 128
BM = 256
ROW_CHUNK = 32
LANES = 128


def _pitch(rows):
    assert rows % 8 == 0
    return rows + 4
VMEM_LIMIT = 56 * 1024 * 1024


def _cparams(sem):
    return pltpu.CompilerParams(dimension_semantics=sem, vmem_limit_bytes=VMEM_LIMIT)


def _silu(x):
    return x * jax.nn.sigmoid(x)


def _layer_norm(y, g, b):
    mu = jnp.mean(y, axis=-1, keepdims=True)
    d = y - mu
    var = jnp.mean(d * d, axis=-1, keepdims=True)
    return d * lax.rsqrt(var + LN_EPS) * g + b


def _load_weight_bf16(w_hbm, w_bf, stage, sem, ck):
    n = w_hbm.shape[0] // ck

    def cp(c, slot):
        return pltpu.make_async_copy(w_hbm.at[pl.ds(pl.multiple_of(c * ck, ck), ck)], stage.at[slot], sem.at[slot])

    cp(0, 0).start()

    def body(c, carry):
        slot = c % 2

        @pl.when(c + 1 < n)
        def _():
            cp(c + 1, 1 - slot).start()

        cp(c, slot).wait()
        w_bf[pl.ds(pl.multiple_of(c * ck, ck), ck), :] = stage[slot].astype(BF16)
        return carry

    lax.fori_loop(0, n, body, 0)


def _mod_specs(l, j, d, tm, n_prompt_tiles, tiles_per_batch, nb):
    s = pl.BlockSpec((None, tm, d), lambda i: (l, jnp.maximum(i - n_prompt_tiles, 0), j))
    p = pl.BlockSpec((None, None, 1, d), lambda i: (l, jnp.minimum(i // tiles_per_batch, nb - 1), 0, j))
    return [s, p]


def _adaln_kernel(c_ref, w_ref, b_ref, os_ref, op_ref, *, ns_rows):
    c = c_ref[...]
    s = _silu(c).astype(BF16)
    r = jnp.dot(s, w_ref[...].astype(BF16), preferred_element_type=F32) + b_ref[...]
    os_ref[...] = r[:ns_rows]
    op_ref[...] = r[ns_rows:]


def _adaln(c_all, ada_w, ada_b, ns_rows):
    depth, d, n6 = ada_w.shape
    cm = c_all.shape[0]
    tn = 1024
    return pl.pallas_call(
        functools.partial(_adaln_kernel, ns_rows=ns_rows),
        grid=(depth, n6 // tn),
        in_specs=[pl.BlockSpec((cm, d), lambda l, j: (0, 0)),
                  pl.BlockSpec((None, d, tn), lambda l, j: (l, 0, j)),
                  pl.BlockSpec((None, 1, tn), lambda l, j: (l, 0, j))],
        out_specs=[pl.BlockSpec((None, ns_rows, tn), lambda l, j: (l, 0, j)),
                   pl.BlockSpec((None, cm - ns_rows, tn), lambda l, j: (l, 0, j))],
        out_shape=[jax.ShapeDtypeStruct((depth, ns_rows, n6), F32),
                   jax.ShapeDtypeStruct((depth, cm - ns_rows, n6), F32)],
        compiler_params=_cparams(("arbitrary", "arbitrary")),
        name="adaln",
    )(c_all, ada_w, ada_b.reshape(depth, 1, n6))


def _inproj_kernel(x_ref, scs_ref, scp_ref, shs_ref, shp_ref, w_hbm, *rest, n_prompt_tiles, nc, odd):
    if odd:
        lng_ref, lnb_ref, u_ref, v_ref, w_bf, hb, stage, sem = rest
    else:
        z_ref, w_bf, hb, stage, sem = rest
    i = pl.program_id(0)

    @pl.when(i == 0)
    def _():
        _load_weight_bf16(w_hbm, w_bf, stage, sem, stage.shape[1])

    is_s = i >= n_prompt_tiles
    for r0 in range(0, hb.shape[0], ROW_CHUNK):
        rs = slice(r0, r0 + ROW_CHUNK)
        sc = jnp.where(is_s, scs_ref[rs, :], scp_ref[...])
        sh = jnp.where(is_s, shs_ref[rs, :], shp_ref[...])
        hb[rs, :] = (x_ref[rs, :] * (1.0 + sc) + sh).astype(BF16)
    n = w_bf.shape[1]
    for c in range(n // nc):
        z = jnp.dot(hb[...], w_bf[:, c * nc:(c + 1) * nc], preferred_element_type=F32)
        if not odd:
            z_ref[:, c * nc:(c + 1) * nc] = z
        else:
            z = jax.nn.gelu(z)
            half = n // 2
            if c * nc < half:
                u_ref[:, c * nc:(c + 1) * nc] = z
            else:
                v_ref[:, c * nc - half:(c + 1) * nc - half] = z
    if odd:
        v_ref[...] = _layer_norm(v_ref[...], lng_ref[...], lnb_ref[...])


def _inproj(x_all, mod_s, mod_p, l, w, n_prompt_tiles, tiles_per_batch, nb, ln=None):
    ntp, d = x_all.shape
    n = w.shape[1]
    odd = ln is not None
    ck, nc = 128, 1024
    specs = [pl.BlockSpec((TM, d), lambda i: (i, 0))]
    specs += _mod_specs(l, 1, d, TM, n_prompt_tiles, tiles_per_batch, nb)
    specs += _mod_specs(l, 0, d, TM, n_prompt_tiles, tiles_per_batch, nb)
    specs += [pl.BlockSpec(memory_space=pl.ANY)]
    args = [x_all, mod_s, mod_p, mod_s, mod_p, w]
    if odd:
        specs += [pl.BlockSpec((1, n // 2), lambda i: (0, 0))] * 2
        args += [ln[0].reshape(1, -1), ln[1].reshape(1, -1)]
        out_specs = [pl.BlockSpec((TM, n // 2), lambda i: (i, 0))] * 2
        out_shape = [jax.ShapeDtypeStruct((ntp, n // 2), F32)] * 2
    else:
        out_specs = pl.BlockSpec((TM, n), lambda i: (i, 0))
        out_shape = jax.ShapeDtypeStruct((ntp, n), F32)
    return pl.pallas_call(
        functools.partial(_inproj_kernel, n_prompt_tiles=n_prompt_tiles, nc=nc, odd=odd),
        grid=(ntp // TM,),
        in_specs=specs, out_specs=out_specs, out_shape=out_shape,
        scratch_shapes=[pltpu.VMEM((d, n), BF16), pltpu.VMEM((TM, d), BF16),
                        pltpu.VMEM((2, ck, n), F32), pltpu.SemaphoreType.DMA((2,))],
        compiler_params=_cparams(("arbitrary",)),
        name="inproj_odd" if odd else "inproj_even",
    )(*args)


def _forget_lower_bound(rows, l):
    m = functools.reduce(jnp.maximum, rows)
    e = [jnp.exp(r - m) for r in rows]
    return sum(e[:l + 1]) / sum(e)


def _gla_prompt_kernel(q_ref, f_ref, v_ref, g_ref, lbt_ref, rms_ref, o_ref, s_ref,
                       st, qe_s, ke_s, kd_s, es_s, o_s, *, l, n_heads, dk):
    tb = pl.program_id(1)
    rows = q_ref.shape[0]
    nchunk = rows // GLA_CHUNK

    @pl.when(tb == 0)
    def _():
        st[...] = jnp.zeros_like(st)

    lb = _forget_lower_bound([lbt_ref[r:r + 1, :] for r in range(lbt_ref.shape[0])], l)
    fp = f_ref[...]
    logf = jnp.log(lb + (1.0 - lb) * jax.nn.sigmoid(fp))
    k = (1.0 - lb) * jax.nn.sigmoid(-fp)
    rid = lax.broadcasted_iota(jnp.int32, logf.shape, 0) % GLA_CHUNK
    bcum = logf
    s = 1
    while s < GLA_CHUNK:
        bcum = bcum + jnp.where(rid >= s, pltpu.roll(bcum, s, axis=0), 0.0)
        s *= 2
    b3 = bcum.reshape(nchunk, GLA_CHUNK, bcum.shape[1])
    blast = b3[:, GLA_CHUNK - 1:GLA_CHUNK, :]
    es_s[...] = jnp.exp(blast).reshape(nchunk, bcum.shape[1])
    blast_full = jnp.broadcast_to(blast, b3.shape).reshape(bcum.shape)
    qe_s[...] = q_ref[...] * jnp.exp(bcum)
    ke_s[...] = k * jnp.exp(-bcum)
    kd_s[...] = k * jnp.exp(blast_full - bcum)

    ri = lax.broadcasted_iota(jnp.int32, (GLA_CHUNK, GLA_CHUNK), 0)
    ci = lax.broadcasted_iota(jnp.int32, (GLA_CHUNK, GLA_CHUNK), 1)
    causal = ri >= ci

    def chunk_step(c, carry):
        r0 = pl.multiple_of(c * GLA_CHUNK, GLA_CHUNK)
        es_row = es_s[pl.ds(c, 1), :]
        for h in range(n_heads):
            cols = slice(h * dk, (h + 1) * dk)
            qe = qe_s[pl.ds(r0, GLA_CHUNK), cols]
            ke = ke_s[pl.ds(r0, GLA_CHUNK), cols]
            kd = kd_s[pl.ds(r0, GLA_CHUNK), cols]
            v = v_ref[pl.ds(r0, GLA_CHUNK), cols]
            att = lax.dot_general(qe, ke, (((1,), (1,)), ((), ())), preferred_element_type=F32)
            att = jnp.where(causal, att, 0.0)
            sth = st[h]
            o = jnp.dot(att, v, preferred_element_type=F32)
            o = o + lax.dot_general(qe, sth, (((1,), (1,)), ((), ())), preferred_element_type=F32)
            upd = lax.dot_general(v, kd, (((0,), (0,)), ((), ())), preferred_element_type=F32)
            st[h] = es_row[:, cols] * sth + upd
            o_s[pl.ds(r0, GLA_CHUNK), cols] = o
        return carry

    lax.fori_loop(0, nchunk, chunk_step, 0)

    for h in range(n_heads):
        cols = slice(h * dk, (h + 1) * dk)
        o = o_s[:, cols]
        o = o * lax.rsqrt(jnp.mean(o * o, axis=-1, keepdims=True) + RMS_EPS)
        o_ref[:, cols] = (o * rms_ref[:, cols] * _silu(g_ref[:, cols])).astype(o_ref.dtype)

    @pl.when(tb == pl.num_programs(1) - 1)
    def _():
        for h in range(n_heads):
            s_ref[0, h] = st[h].T


def _gla_prompt(z, lb_table, rms_w, l, nb, t, n_heads, dk):
    d_a = n_heads * dk
    tbk = 512
    nt = t // tbk
    row = lambda b, tb: b * nt + tb
    zspec = lambda sec: pl.BlockSpec((tbk, d_a), lambda b, tb: (row(b, tb), sec))
    return pl.pallas_call(
        functools.partial(_gla_prompt_kernel, l=l, n_heads=n_heads, dk=dk),
        grid=(nb, nt),
        in_specs=[zspec(0), zspec(1), zspec(2), zspec(3),
                  pl.BlockSpec(lb_table.shape, lambda b, tb: (0, 0)),
                  pl.BlockSpec((1, d_a), lambda b, tb: (0, 0))],
        out_specs=[pl.BlockSpec((tbk, d_a), lambda b, tb: (row(b, tb), 0)),
                   pl.BlockSpec((1, n_heads, dk, dk), lambda b, tb: (b, 0, 0, 0))],
        out_shape=[jax.ShapeDtypeStruct((nb * t, d_a), BF16),
                   jax.ShapeDtypeStruct((nb, n_heads, dk, dk), F32)],
        scratch_shapes=[pltpu.VMEM((n_heads, dk, dk), F32),
                        pltpu.VMEM((tbk, d_a), F32), pltpu.VMEM((tbk, d_a), F32), pltpu.VMEM((tbk, d_a), F32),
                        pltpu.VMEM((tbk // GLA_CHUNK, d_a), F32), pltpu.VMEM((tbk, d_a), F32)],
        compiler_params=_cparams(("arbitrary", "arbitrary")),
        name="gla_prompt",
    )(z, z, z, z, lb_table, rms_w.reshape(1, d_a))


def _gla_sample_kernel(qt_ref, ft_ref, lbt_ref, v_ref, g_ref, rms_ref, s0_ref, o_ref, s_ref, *, l):
    ns = s0_ref.shape[0]
    lbc = _forget_lower_bound([lbt_ref[r] for r in range(lbt_ref.shape[0])], l)
    fp = ft_ref[...]
    fcol = lbc + (1.0 - lbc) * jax.nn.sigmoid(fp)
    kcol = (1.0 - lbc) * jax.nn.sigmoid(-fp)
    qcol = qt_ref[...]
    for b in range(ns):
        v = v_ref[b:b + 1, :]
        s_new = fcol[:, b:b + 1] * s0_ref[b, 0] + kcol[:, b:b + 1] * v
        s_ref[b, 0] = s_new
        o_ref[b:b + 1, :] = jnp.sum(qcol[:, b:b + 1] * s_new, axis=0, keepdims=True)
    o = o_ref[...]
    o = o * lax.rsqrt(jnp.mean(o * o, axis=-1, keepdims=True) + RMS_EPS)
    o_ref[...] = o * rms_ref[...] * _silu(g_ref[...])


def _gla_sample(qt, ft, lbt_col, v_s, g_s, rms_w, s0, l):
    ns, n_heads, dk, dv = s0.shape
    return pl.pallas_call(
        functools.partial(_gla_sample_kernel, l=l),
        grid=(n_heads,),
        in_specs=[pl.BlockSpec((None, dk, ns), lambda h: (h, 0, 0)),
                  pl.BlockSpec((None, dk, ns), lambda h: (h, 0, 0)),
                  pl.BlockSpec((lbt_col.shape[0], None, dk, 1), lambda h: (0, h, 0, 0)),
                  pl.BlockSpec((ns, dv), lambda h: (0, h)),
                  pl.BlockSpec((ns, dv), lambda h: (0, h)),
                  pl.BlockSpec((1, dv), lambda h: (0, h)),
                  pl.BlockSpec((ns, 1, dk, dv), lambda h: (0, h, 0, 0))],
        out_specs=[pl.BlockSpec((ns, dv), lambda h: (0, h)),
                   pl.BlockSpec((ns, 1, dk, dv), lambda h: (0, h, 0, 0))],
        out_shape=[jax.ShapeDtypeStruct((ns, n_heads * dv), F32),
                   jax.ShapeDtypeStruct(s0.shape, F32)],
        compiler_params=_cparams(("arbitrary",)),
        name="gla_sample",
    )(qt, ft, lbt_col, v_s, g_s, rms_w.reshape(1, -1), s0)


def _pool_mix(win_sums, u, inv_cnt, pw_ref, ps_ref, o_ref, gb):
    for gi in range(len(POOL_WINDOWS)):
        cols = slice(gi * gb, (gi + 1) * gb)
        pooled = win_sums[gi] * inv_cnt[gi] - u[:, cols]
        mixed = jnp.dot(pooled.astype(BF16), pw_ref[gi].astype(BF16), preferred_element_type=F32)
        o_ref[:, cols] = (mixed * ps_ref[:, cols]).astype(o_ref.dtype)


def _pool_prompt_kernel(u_ref, pw_ref, ps_ref, o_ref, tail_ref, buf, *, hist_pad):
    tb = pl.program_id(1)
    rows, d_b = u_ref.shape
    gb = d_b // len(POOL_WINDOWS)

    @pl.when(tb == 0)
    def _():
        buf[0:hist_pad, :] = jnp.zeros((hist_pad, d_b), F32)

    u = u_ref[...]
    buf[hist_pad:, :] = u
    acc = buf[...]
    sums = []
    span = 1
    for gi, w in enumerate(POOL_WINDOWS):
        while span < w:
            acc = acc + pltpu.roll(acc, span, axis=0)
            span *= 2
        sums.append(acc[hist_pad:, gi * gb:(gi + 1) * gb])
    pos = tb * rows + lax.broadcasted_iota(jnp.int32, (rows, gb), 0)
    inv_cnt = [1.0 / jnp.minimum(pos + 1, w).astype(F32) for w in POOL_WINDOWS]
    _pool_mix(sums, u, inv_cnt, pw_ref, ps_ref, o_ref, gb)
    buf[0:hist_pad, :] = u[rows - hist_pad:, :]
    tail_ref[0] = u[rows - hist_pad:, :]


def _pool_prompt(z, pool_w, pool_scale, nb, t, d_b, u_col_block, hist_pad):
    tbk = 512
    nt = t // tbk
    return pl.pallas_call(
        functools.partial(_pool_prompt_kernel, hist_pad=hist_pad),
        grid=(nb, nt),
        in_specs=[pl.BlockSpec((tbk, d_b), lambda b, tb: (b * nt + tb, u_col_block)),
                  pl.BlockSpec(pool_w.shape, lambda b, tb: (0, 0, 0)),
                  pl.BlockSpec((1, d_b), lambda b, tb: (0, 0))],
        out_specs=[pl.BlockSpec((tbk, d_b), lambda b, tb: (b * nt + tb, 0)),
                   pl.BlockSpec((1, hist_pad, d_b), lambda b, tb: (b, 0, 0))],
        out_shape=[jax.ShapeDtypeStruct((nb * t, d_b), BF16),
                   jax.ShapeDtypeStruct((nb, hist_pad, d_b), F32)],
        scratch_shapes=[pltpu.VMEM((hist_pad + tbk, d_b), F32)],
        compiler_params=_cparams(("arbitrary", "arbitrary")),
        name="pool_prompt",
    )(z, pool_w, pool_scale.reshape(1, d_b))


def _pool_sample_kernel(hist_ref, u_ref, pw_ref, ps_ref, o_ref, nb_ref, *, n_hist):
    ns, d_b = u_ref.shape
    gb = d_b // len(POOL_WINDOWS)
    u = u_ref[...]
    row = lambda j: hist_ref[:, j * d_b:(j + 1) * d_b]
    sums, inv_cnt = [], []
    for gi, w in enumerate(POOL_WINDOWS):
        cols = slice(gi * gb, (gi + 1) * gb)
        acc = u[:, cols]
        for j in range(n_hist - (w - 1), n_hist):
            acc = acc + row(j)[:, cols]
        sums.append(acc)
        inv_cnt.append(1.0 / float(min(PAST_LEN + 1, w)))
    _pool_mix(sums, u, inv_cnt, pw_ref, ps_ref, o_ref, gb)
    nb_ref[:, 0:(n_hist - 1) * d_b] = hist_ref[:, d_b:]
    nb_ref[:, (n_hist - 1) * d_b:] = u


def _pool_sample(hist2d, z, pool_w, pool_scale, ns, d_b, u_row_block, u_col_block, n_hist):
    return pl.pallas_call(
        functools.partial(_pool_sample_kernel, n_hist=n_hist),
        grid=(1,),
        in_specs=[pl.BlockSpec(hist2d.shape, lambda i: (0, 0)),
                  pl.BlockSpec((ns, d_b), lambda i: (u_row_block, u_col_block)),
                  pl.BlockSpec(pool_w.shape, lambda i: (0, 0, 0)),
                  pl.BlockSpec((1, d_b), lambda i: (0, 0))],
        out_specs=[pl.BlockSpec((ns, d_b), lambda i: (0, 0)),
                   pl.BlockSpec(hist2d.shape, lambda i: (0, 0))],
        out_shape=[jax.ShapeDtypeStruct((ns, d_b), BF16),
                   jax.ShapeDtypeStruct(hist2d.shape, F32)],
        compiler_params=_cparams(("arbitrary",)),
        name="pool_sample",
    )(hist2d, z, pool_w, pool_scale.reshape(1, d_b))


def _spatial_kernel(u_ref, v_ref, wsp_ref, bt_ref, w00_ref, b0_ref, o_ref, *, n_prompt_tiles, n_heads):
    i = pl.program_id(0)
    rows, d_c = u_ref.shape
    gc = d_c // n_heads

    @pl.when(i < n_prompt_tiles)
    def _():
        ri = lax.broadcasted_iota(jnp.int32, (C_CHUNK, C_CHUNK), 0)
        ci = lax.broadcasted_iota(jnp.int32, (C_CHUNK, C_CHUNK), 1)
        for h in range(n_heads):
            wc = jnp.where(ri >= ci, wsp_ref[h], 0.0).astype(BF16)
            bias = bt_ref[:, h:h + 1]
            for c in range(rows // C_CHUNK):
                rs = slice(c * C_CHUNK, (c + 1) * C_CHUNK)
                cs = slice(h * gc, (h + 1) * gc)
                mixed = jnp.dot(wc, v_ref[rs, cs].astype(BF16), preferred_element_type=F32) + bias
                o_ref[rs, cs] = (u_ref[rs, cs] * mixed).astype(o_ref.dtype)

    @pl.when(i >= n_prompt_tiles)
    def _():
        o_ref[...] = (u_ref[...] * (v_ref[...] * w00_ref[...] + b0_ref[...])).astype(o_ref.dtype)


def _spatial(u, v, w_sp, b_sp, n_prompt_tiles):
    ntp, d_c = u.shape
    n_heads = w_sp.shape[0]
    gc = d_c // n_heads
    w00 = jnp.repeat(w_sp[:, 0, 0], gc).reshape(1, d_c)
    b0 = jnp.repeat(b_sp[:, 0], gc).reshape(1, d_c)
    tile = pl.BlockSpec((TM, d_c), lambda i: (i, 0))
    return pl.pallas_call(
        functools.partial(_spatial_kernel, n_prompt_tiles=n_prompt_tiles, n_heads=n_heads),
        grid=(ntp // TM,),
        in_specs=[tile, tile,
                  pl.BlockSpec(w_sp.shape, lambda i: (0, 0, 0)),
                  pl.BlockSpec((C_CHUNK, n_heads), lambda i: (0, 0)),
                  pl.BlockSpec((1, d_c), lambda i: (0, 0)),
                  pl.BlockSpec((1, d_c), lambda i: (0, 0))],
        out_specs=tile,
        out_shape=jax.ShapeDtypeStruct((ntp, d_c), BF16),
        compiler_params=_cparams(("arbitrary",)),
        name="spatial",
    )(u, v, w_sp, b_sp.T, w00, b0)


def _outproj_kernel(*refs, n_lhs, n_prompt_tiles, alpha):
    lhs = refs[:n_lhs]
    (x_ref, g1s, g1p, sc2s, sc2p, sh2s, sh2p, lng_ref, lnb_ref, wr_ref, w_hbm,
     x1_ref, h2_ref, lg_ref, h2r_ref, w_bf, stage, sem) = refs[n_lhs:]
    i = pl.program_id(0)

    @pl.when(i == 0)
    def _():
        _load_weight_bf16(w_hbm, w_bf, stage, sem, stage.shape[1])

    is_s = i >= n_prompt_tiles
    out = None
    k0 = 0
    for a_ref in lhs:
        kk = a_ref.shape[1]
        part = jnp.dot(a_ref[...], w_bf[k0:k0 + kk, :], preferred_element_type=F32)
        out = part if out is None else out + part
        k0 += kk
    for r0 in range(0, out.shape[0], ROW_CHUNK):
        rs = slice(r0, r0 + ROW_CHUNK)
        g1 = jnp.where(is_s, g1s[rs, :], g1p[...])
        x1 = _layer_norm(alpha * x_ref[rs, :] + g1 * out[rs], lng_ref[...], lnb_ref[...])
        x1_ref[rs, :] = x1
        sc2 = jnp.where(is_s, sc2s[rs, :], sc2p[...])
        sh2 = jnp.where(is_s, sh2s[rs, :], sh2p[...])
        h2_ref[rs, :] = x1 * (1.0 + sc2) + sh2
    lg_ref[...] = lax.dot_general(wr_ref[...], h2_ref[...], (((1,), (1,)), ((), ())),
                                  precision=lax.Precision.HIGHEST, preferred_element_type=F32)
    pitch = h2r_ref.shape[0] // h2_ref.shape[0]
    rpt = h2_ref.shape[1] // LANES
    for j in range(rpt):
        h2r_ref[pl.ds(j, h2_ref.shape[0], stride=pitch), :] = h2_ref[:, j * LANES:(j + 1) * LANES]
    for j in range(rpt, pitch):
        h2r_ref[pl.ds(j, h2_ref.shape[0], stride=pitch), :] = jnp.zeros((h2_ref.shape[0], LANES), F32)


def _outproj(lhs, x_all, mod_s, mod_p, l, w_out, ln_g, ln_b, w_router_t, n_prompt_tiles, tiles_per_batch, nb, alpha):
    ntp, d = x_all.shape
    n_e = w_router_t.shape[0]
    ck = 256
    tile = pl.BlockSpec((TM, d), lambda i: (i, 0))
    specs = [pl.BlockSpec((TM, a.shape[1]), lambda i: (i, 0)) for a in lhs] + [tile]
    for j in (2, 4, 3):
        specs += _mod_specs(l, j, d, TM, n_prompt_tiles, tiles_per_batch, nb)
    specs += [pl.BlockSpec((1, d), lambda i: (0, 0))] * 2
    specs += [pl.BlockSpec((n_e, d), lambda i: (0, 0)), pl.BlockSpec(memory_space=pl.ANY)]
    return pl.pallas_call(
        functools.partial(_outproj_kernel, n_lhs=len(lhs), n_prompt_tiles=n_prompt_tiles, alpha=alpha),
        grid=(ntp // TM,),
        in_specs=specs,
        out_specs=[tile, tile, pl.BlockSpec((n_e, TM), lambda i: (0, i)),
                   pl.BlockSpec((TM * _pitch(d // LANES), LANES), lambda i: (i, 0))],
        out_shape=[jax.ShapeDtypeStruct((ntp, d), F32), jax.ShapeDtypeStruct((ntp, d), F32),
                   jax.ShapeDtypeStruct((n_e, ntp), F32),
                   jax.ShapeDtypeStruct((ntp * _pitch(d // LANES), LANES), F32)],
        scratch_shapes=[pltpu.VMEM(w_out.shape, BF16), pltpu.VMEM((2, ck, d), F32), pltpu.SemaphoreType.DMA((2,))],
        compiler_params=_cparams(("arbitrary",)),
        name="outproj",
    )(*lhs, x_all, *([mod_s, mod_p] * 3), ln_g.reshape(1, d), ln_b.reshape(1, d), w_router_t, w_out)


def _first_argmax(vals, iota, axis, size):
    m = jnp.max(vals, axis=axis, keepdims=True)
    idx = jnp.min(jnp.where(vals == m, iota, size), axis=axis, keepdims=True)
    return m, idx


def _route_kernel(lg_ref, bias_ref, e_ref, w_ref):
    n_e, tm = lg_ref.shape
    gsz = n_e // N_GROUPS
    neg = -jnp.inf
    scores = jax.nn.sigmoid(lg_ref[...])
    sel = scores + bias_ref[...]
    sel3 = sel.reshape(N_GROUPS, gsz, tm)
    io3 = lax.broadcasted_iota(jnp.int32, sel3.shape, 1)
    m1, i1 = _first_argmax(sel3, io3, 1, gsz)
    m2 = jnp.max(jnp.where(io3 == i1, neg, sel3), axis=1, keepdims=True)
    gscore = (m1 + m2).reshape(N_GROUPS, tm)
    iog = lax.broadcasted_iota(jnp.int32, gscore.shape, 0)
    gsel = jnp.zeros(gscore.shape, F32)
    cur = gscore
    for _ in range(TOPK_GROUPS):
        _, gi = _first_argmax(cur, iog, 0, N_GROUPS)
        pick = iog == gi
        gsel = jnp.where(pick, 1.0, gsel)
        cur = jnp.where(pick, neg, cur)
    esel = jnp.broadcast_to(gsel.reshape(N_GROUPS, 1, tm), sel3.shape).reshape(n_e, tm)
    ioe = lax.broadcasted_iota(jnp.int32, sel.shape, 0)
    iok = lax.broadcasted_iota(jnp.int32, (TOP_K, tm), 0)
    cur = jnp.where(esel > 0.5, sel, neg)
    eidx = jnp.zeros((TOP_K, tm), jnp.int32)
    sc = jnp.zeros((TOP_K, tm), F32)
    for r in range(TOP_K):
        _, ei = _first_argmax(cur, ioe, 0, n_e)
        pick = ioe == ei
        val = jnp.sum(jnp.where(pick, scores, 0.0), axis=0, keepdims=True)
        eidx = jnp.where(iok == r, ei, eidx)
        sc = jnp.where(iok == r, val, sc)
        cur = jnp.where(pick, neg, cur)
    e_ref[...] = eidx
    w_ref[...] = sc / jnp.sum(sc, axis=0, keepdims=True) * ROUTED_SCALE


def _route(logits_t, router_bias):
    n_e, ntp = logits_t.shape
    return pl.pallas_call(
        _route_kernel,
        grid=(ntp // TM,),
        in_specs=[pl.BlockSpec((n_e, TM), lambda i: (0, i)), pl.BlockSpec((n_e, 1), lambda i: (0, 0))],
        out_specs=[pl.BlockSpec((TOP_K, TM), lambda i: (0, i))] * 2,
        out_shape=[jax.ShapeDtypeStruct((TOP_K, ntp), jnp.int32), jax.ShapeDtypeStruct((TOP_K, ntp), F32)],
        compiler_params=_cparams(("arbitrary",)),
        name="route",
    )(logits_t, router_bias.reshape(n_e, 1))


def _dispatch_tables(eidx_t, n_e):
    flat_e = eidx_t.T.reshape(-1)
    nk = flat_e.shape[0]
    n_steps = nk // BM + n_e + 1
    abits = (nk + BM - 1).bit_length()
    experts = jnp.arange(n_e, dtype=jnp.int32)
    counts = jnp.sum((flat_e[:, None] == experts[None, :]).astype(jnp.int32), axis=0)
    nblk = (counts + BM - 1) // BM
    bends = jnp.cumsum(nblk)
    total = bends[-1]
    j = jnp.arange(BM, dtype=jnp.int32)
    real = jnp.left_shift(flat_e, abits) | jnp.arange(nk, dtype=jnp.int32)
    pad = jnp.where(j[None, :] < (nblk * BM - counts)[:, None],
                    jnp.left_shift(experts, abits)[:, None] | (nk + j)[None, :], jnp.iinfo(jnp.int32).max)
    ids = jnp.sort(jnp.concatenate([real, pad.reshape(-1)])) & ((1 << abits) - 1)
    table = jnp.concatenate([nk + j, ids, nk + j])
    g = jnp.arange(n_steps, dtype=jnp.int32)
    e_of = jnp.sum((bends[None, :] <= g[:, None]).astype(jnp.int32), axis=1)
    e_last = jnp.sum((bends <= total - 1).astype(jnp.int32))
    blk_e = jnp.minimum(jnp.where(g < total, e_of, e_last), n_e - 1)
    return blk_e.astype(jnp.int32), total.reshape(1).astype(jnp.int32), table


def _experts_kernel(blk_e, total_ref, table, h2_hbm, wg_ref, wu_ref, wd_ref, y_hbm,
                    xbuf, ybuf, xb, hb, wg_bf, wu_bf, wd_bf, gsem, ssem, *, ntp, nc):
    g = pl.program_id(0)
    total = total_ref[0]
    s = g % 2
    kbits = TOP_K.bit_length() - 1
    d = xb.shape[1]
    rpt = d // LANES
    pitch = xbuf.shape[1] // BM

    def gather(j, slot, r):
        a = table[(j + 1) * BM + r]
        tok = jnp.minimum(lax.shift_right_logical(a, kbits), ntp - 1)
        return pltpu.make_async_copy(h2_hbm.at[pl.ds(tok * pitch, rpt)], xbuf.at[slot, pl.ds(r * pitch, rpt)],
                                     gsem.at[slot])

    def scatter(j, slot, r):
        a = table[(j + 1) * BM + r]
        return pltpu.make_async_copy(ybuf.at[slot, pl.ds(r * pitch, rpt)],
                                     y_hbm.at[pl.ds(pl.multiple_of(a * rpt, rpt), rpt)], ssem.at[slot])

    def wait_gather(slot):
        for r in range(BM):
            pltpu.make_async_copy(h2_hbm.at[pl.ds(0, rpt)], xbuf.at[slot, pl.ds(r * pitch, rpt)], gsem.at[slot]).wait()

    def wait_scatter(slot):
        for r in range(BM):
            pltpu.make_async_copy(ybuf.at[slot, pl.ds(r * pitch, rpt)], y_hbm.at[pl.ds(0, rpt)], ssem.at[slot]).wait()

    @pl.when(g == 0)
    def _():
        ybuf[1] = jnp.zeros(ybuf.shape[1:], F32)

        def body(r, c):
            gather(0, 0, r).start()
            return c
        lax.fori_loop(0, BM, body, 0, unroll=8)

    @pl.when(g < total)
    def _():
        prev = blk_e[jnp.maximum(g - 1, 0)]

        @pl.when(jnp.logical_or(g == 0, blk_e[g] != prev))
        def _():
            wg_bf[...] = wg_ref[...].astype(BF16)
            wu_bf[...] = wu_ref[...].astype(BF16)
            wd_bf[...] = wd_ref[...].astype(BF16)

        wait_gather(s)

        @pl.when(g >= 1)
        def _():
            wait_scatter(s)

        d_e = wd_bf.shape[0]
        pieces = d_e // nc + d // nc
        rows = [list(range(p * BM // pieces, (p + 1) * BM // pieces)) for p in range(pieces)]

        def issue(p):
            for r in rows[p]:
                gather(g + 1, 1 - s, r).start()
                scatter(g - 1, 1 - s, r).start(priority=1)

        for c in range(rpt):
            xb[:, c * LANES:(c + 1) * LANES] = xbuf[s, pl.ds(c, BM, stride=pitch), :].astype(BF16)
        p = 0
        for c in range(d_e // nc):
            cols = slice(c * nc, (c + 1) * nc)
            hg = jnp.dot(xb[...], wg_bf[:, cols], preferred_element_type=F32)
            hu = jnp.dot(xb[...], wu_bf[:, cols], preferred_element_type=F32)
            hb[:, cols] = (_silu(hg) * hu).astype(BF16)
            issue(p)
            p += 1
        for c in range(d // nc):
            y = jnp.dot(hb[...], wd_bf[:, c * nc:(c + 1) * nc], preferred_element_type=F32)
            for cc in range(nc // LANES):
                ybuf[s, pl.ds(c * (nc // LANES) + cc, BM, stride=pitch), :] = y[:, cc * LANES:(cc + 1) * LANES]
            issue(p)
            p += 1

    @pl.when(g == total)
    def _():
        wait_gather(s)
        wait_scatter(s)

        def body(r, c):
            scatter(g - 1, 1 - s, r).start(priority=1)
            return c
        lax.fori_loop(0, BM, body, 0, unroll=8)
        wait_scatter(1 - s)


def _experts(h2r, tables, w_gate, w_up, w_down, l):
    d, d_e = w_gate.shape[-2:]
    rpt = d // LANES
    pitch = _pitch(rpt)
    ntp = h2r.shape[0] // pitch
    n_steps = tables[0].shape[0]
    wmap = lambda g, be, *_: (l, be[g], 0, 0)
    return pl.pallas_call(
        functools.partial(_experts_kernel, ntp=ntp, nc=256),
        grid_spec=pltpu.PrefetchScalarGridSpec(
            num_scalar_prefetch=3, grid=(n_steps,),
            in_specs=[pl.BlockSpec(memory_space=pl.ANY),
                      pl.BlockSpec((None, None, d, d_e), wmap),
                      pl.BlockSpec((None, None, d, d_e), wmap),
                      pl.BlockSpec((None, None, d_e, d), wmap)],
            out_specs=pl.BlockSpec(memory_space=pl.ANY),
            scratch_shapes=[pltpu.VMEM((2, BM * pitch, LANES), F32), pltpu.VMEM((2, BM * pitch, LANES), F32),
                            pltpu.VMEM((BM, d), BF16), pltpu.VMEM((BM, d_e), BF16),
                            pltpu.VMEM((d, d_e), BF16), pltpu.VMEM((d, d_e), BF16), pltpu.VMEM((d_e, d), BF16),
                            pltpu.SemaphoreType.DMA((2,)), pltpu.SemaphoreType.DMA((2,))]),
        out_shape=jax.ShapeDtypeStruct(((ntp * TOP_K + BM) * rpt, LANES), F32),
        compiler_params=_cparams(("arbitrary",)),
        name="experts",
    )(*tables, h2r, w_gate, w_up, w_down)


def _final_kernel(*refs, l, n_prompt_tiles, alpha, split):
    (y_hbm, w_ref, x1_ref, h2_ref, g2s, g2p, lng_ref, lnb_ref, wg_hbm, wu_hbm, wd_hbm) = refs[:11]
    if split:
        yp_ref, ysm_ref = refs[11:13]
        wg_bf, wu_bf, wd_bf, stage_a, stage_b, sem, ybuf, ysem, ffn_s, x2_s = refs[13:]
    else:
        x2_ref = refs[11]
        wg_bf, wu_bf, wd_bf, stage_a, stage_b, sem, ybuf, ysem, ffn_s = refs[12:]
    i = pl.program_id(0)
    tm, d = x1_ref.shape
    rpt = d // LANES
    rec = TOP_K * rpt
    pitch = ybuf.shape[1] // tm
    slot = i % 2

    def fetch(tile, sl, t):
        src = y_hbm.at[pl.ds(pl.multiple_of((tile * tm + t) * rec, rec), rec)]
        return pltpu.make_async_copy(src, ybuf.at[sl, pl.ds(t * pitch, rec)], ysem.at[sl])

    @pl.when(i == 0)
    def _():
        for t in range(tm):
            fetch(0, 0, t).start()
        _load_weight_bf16(wg_hbm.at[l], wg_bf, stage_a, sem, stage_a.shape[1])
        _load_weight_bf16(wu_hbm.at[l], wu_bf, stage_a, sem, stage_a.shape[1])
        _load_weight_bf16(wd_hbm.at[l], wd_bf, stage_b, sem, stage_b.shape[1])

    @pl.when(i + 1 < pl.num_programs(0))
    def _():
        for t in range(tm):
            fetch(i + 1, 1 - slot, t).start()

    hb = h2_ref[...].astype(BF16)
    sg = jnp.dot(hb, wg_bf[...], preferred_element_type=F32)
    su = jnp.dot(hb, wu_bf[...], preferred_element_type=F32)
    shared = jnp.dot((_silu(sg) * su).astype(BF16), wd_bf[...], preferred_element_type=F32)
    is_s = i >= n_prompt_tiles
    dst = x2_s if split else x2_ref

    for t in range(tm):
        fetch(0, slot, t).wait()
    w = w_ref[...]
    wk = [jnp.broadcast_to(w[:, k:k + 1], (tm, LANES)) for k in range(TOP_K)]
    for c in range(rpt):
        acc = shared[:, c * LANES:(c + 1) * LANES]
        for k in range(TOP_K):
            acc = acc + wk[k] * ybuf[slot, pl.ds(k * rpt + c, tm, stride=pitch), :]
        ffn_s[:, c * LANES:(c + 1) * LANES] = acc

    for r0 in range(0, tm, ROW_CHUNK):
        rs = slice(r0, r0 + ROW_CHUNK)
        g2 = jnp.where(is_s, g2s[rs, :], g2p[...])
        dst[rs, :] = _layer_norm(alpha * x1_ref[rs, :] + g2 * ffn_s[rs, :], lng_ref[...], lnb_ref[...])

    if split:
        @pl.when(i < n_prompt_tiles)
        def _():
            yp_ref[...] = x2_s[...]

        @pl.when(i == n_prompt_tiles)
        def _():
            ysm_ref[...] = x2_s[...]


def _final(yslots, wsel, x1, h2, mod_s, mod_p, l, ln_g, ln_b, ws_gate, ws_up, ws_down,
           n_prompt_rows, rows_per_batch, nb, alpha, split):
    ntp, d = x1.shape
    tm = TM_FINAL
    d_s = ws_gate.shape[-1]
    npt = n_prompt_rows // tm
    tpb = rows_per_batch // tm
    nt = ntp // tm
    tile = pl.BlockSpec((tm, d), lambda i: (i, 0))
    specs = [pl.BlockSpec(memory_space=pl.ANY), pl.BlockSpec((tm, TOP_K), lambda i: (i, 0)), tile, tile]
    specs += _mod_specs(l, 5, d, tm, npt, tpb, nb)
    specs += [pl.BlockSpec((1, d), lambda i: (0, 0))] * 2
    specs += [pl.BlockSpec(memory_space=pl.ANY)] * 3
    if split:
        out_specs = [pl.BlockSpec((tm, d), lambda i: (jnp.minimum(i, npt - 1), 0)),
                     pl.BlockSpec((tm, d), lambda i: (0, 0))]
        out_shape = [jax.ShapeDtypeStruct((n_prompt_rows, d), F32), jax.ShapeDtypeStruct((tm, d), F32)]
    else:
        out_specs = tile
        out_shape = jax.ShapeDtypeStruct((ntp, d), F32)
    ck = 256
    scratch = [pltpu.VMEM((d, d_s), BF16), pltpu.VMEM((d, d_s), BF16), pltpu.VMEM((d_s, d), BF16),
               pltpu.VMEM((2, ck, d_s), F32), pltpu.VMEM((2, ck, d), F32), pltpu.SemaphoreType.DMA((2,)),
               pltpu.VMEM((2, tm * _pitch(TOP_K * d // LANES), LANES), F32), pltpu.SemaphoreType.DMA((2,)),
               pltpu.VMEM((tm, d), F32)]
    if split:
        scratch.append(pltpu.VMEM((tm, d), F32))
    return pl.pallas_call(
        functools.partial(_final_kernel, l=l, n_prompt_tiles=npt, alpha=alpha, split=split),
        grid=(nt,),
        in_specs=specs, out_specs=out_specs, out_shape=out_shape,
        scratch_shapes=scratch,
        compiler_params=_cparams(("arbitrary",)),
        name="moe_final",
    )(yslots, wsel, x1, h2, mod_s, mod_p, ln_g.reshape(1, d), ln_b.reshape(1, d),
      ws_gate, ws_up, ws_down)


def kernel(x_prompt, x_sample, state_hgrn, state_pool, c_prompt, c_sample, mix0_w_in, mix0_w_out, hgrn_lb_table, hgrn_rms_w, pool_w, pool_scale, mix1_w_in, gmlp_ln_g, gmlp_ln_b, gmlp_w_sp, gmlp_b_sp, mix1_w_out, ada_w, ada_b, ln_mix_g, ln_mix_b, ln_ffn_g, ln_ffn_b, moe_w_router, moe_router_bias, moe_w_gate, moe_w_up, moe_w_down, shared_w_gate, shared_w_up, shared_w_down):
    nb, t, d = x_prompt.shape
    ns = x_sample.shape[0]
    depth = ada_w.shape[0]
    n_heads, dk = state_hgrn.shape[2], state_hgrn.shape[3]
    d_a = n_heads * dk
    d_b = state_pool.shape[3]
    n_hist = state_pool.shape[2]
    hist_pad = 16
    n_e = moe_w_router.shape[2]
    alpha = (2.0 * depth) ** 0.25
    n_p = nb * t
    npt = n_p // TM
    tpb = t // TM
    assert t % 512 == 0 and ns <= TM and ns % 8 == 0 and n_hist < hist_pad and d_a % d_b == 0

    x_all = jnp.concatenate([x_prompt.reshape(n_p, d), x_sample.reshape(ns, d), jnp.zeros((TM - ns, d), F32)], axis=0)
    c_all = jnp.concatenate([c_sample, jnp.zeros((TM - ns, d), F32), c_prompt, jnp.zeros((8 - nb % 8, d), F32)], axis=0)
    mod_s, mod_p = _adaln(c_all, ada_w, ada_b, TM)
    mod_p = mod_p.reshape(depth, mod_p.shape[1], 1, N_MOD * d)

    hgrn_p, pool_p, hgrn_s, pool_s, chunk_v = [], [], [], [], []
    x = x_all
    for l in range(depth):
        if l % 2 == 0:
            e = l // 2
            z = _inproj(x, mod_s, mod_p, l, mix0_w_in[e], npt, tpb, nb)
            o_a_p, s_p = _gla_prompt(z, hgrn_lb_table, hgrn_rms_w[e], l, nb, t, n_heads, dk)
            o_b_p, tail = _pool_prompt(z, pool_w[e], pool_scale[e], nb, t, d_b, 4 * d_a // d_b, hist_pad)
            hgrn_p.append(s_p)
            pool_p.append(tail[:, hist_pad - n_hist:])
            zs = z[n_p:n_p + ns]
            qt = zs[:, 0:d_a].reshape(ns, n_heads, dk).transpose(1, 2, 0)
            ft = zs[:, d_a:2 * d_a].reshape(ns, n_heads, dk).transpose(1, 2, 0)
            lbt_col = hgrn_lb_table.reshape(-1, n_heads, dk, 1)
            o_a_s, s_s = _gla_sample(qt, ft, lbt_col, zs[:, 2 * d_a:3 * d_a], zs[:, 3 * d_a:4 * d_a],
                                     hgrn_rms_w[e], state_hgrn[e], l)
            o_b_s, buf_s = _pool_sample(state_pool[e].reshape(ns, n_hist * d_b), z, pool_w[e], pool_scale[e],
                                        ns, d_b, n_p // ns, 4 * d_a // d_b, n_hist)
            hgrn_s.append(s_s)
            pool_s.append(buf_s.reshape(ns, n_hist, d_b))
            pad = jnp.zeros((TM - ns, d_a), BF16)
            o_a = jnp.concatenate([o_a_p, o_a_s.astype(BF16), pad], axis=0)
            o_b = jnp.concatenate([o_b_p, o_b_s, jnp.zeros((TM - ns, d_b), BF16)], axis=0)
            lhs, w_out = [o_a, o_b], mix0_w_out[e]
        else:
            o = l // 2
            u, v = _inproj(x, mod_s, mod_p, l, mix1_w_in[o], npt, tpb, nb, ln=(gmlp_ln_g[o], gmlp_ln_b[o]))
            chunk_v.append(v[n_p:n_p + ns].reshape(ns, 1, -1))
            lhs, w_out = [_spatial(u, v, gmlp_w_sp[o], gmlp_b_sp[o], npt)], mix1_w_out[o]
        x1, h2, logits_t, h2r = _outproj(lhs, x, mod_s, mod_p, l, w_out, ln_mix_g[l], ln_mix_b[l],
                                    moe_w_router[l].T, npt, tpb, nb, alpha)
        eidx_t, wsel_t = _route(logits_t, moe_router_bias[l])
        yslots = _experts(h2r, _dispatch_tables(eidx_t, n_e), moe_w_gate, moe_w_up, moe_w_down, l)
        last = l == depth - 1
        res = _final(yslots, wsel_t.T, x1, h2, mod_s, mod_p, l, ln_ffn_g[l], ln_ffn_b[l],
                     shared_w_gate, shared_w_up, shared_w_down, n_p, t, nb, alpha, split=last)
        if last:
            y_p, y_s = res
        else:
            x = res
    return (y_p.reshape(nb, t, d), y_s[:ns].reshape(ns, 1, d),
            jnp.stack(hgrn_p), jnp.stack(pool_p), jnp.stack(hgrn_s), jnp.stack(pool_s), jnp.stack(chunk_v))
```

```python
import functools

import jax
import jax.numpy as jnp
from jax import lax
from jax.experimental import pallas as pl
from jax.experimental.pallas import tpu as pltpu

F32 = jnp.float32
BF16 = jnp.bfloat16

PAST_LEN = 16384
GLA_CHUNK = 32
POOL_WINDOWS = (2, 4, 8, 16)
C_CHUNK = 128
TOP_K = 8
N_GROUPS = 8
TOPK_GROUPS = 4
ROUTED_SCALE = 2.5
LN_EPS = 1e-5
RMS_EPS = 1e-6
N_MOD = 6

TM = 256
TM_FINAL = 128
BM = 256
NSLOT = 3
ROW_CHUNK = 32
LANES = 128


def _pitch(rows):
    assert rows % 8 == 0
    return rows + 4
VMEM_LIMIT = 56 * 1024 * 1024


def _cparams(sem):
    return pltpu.CompilerParams(dimension_semantics=sem, vmem_limit_bytes=VMEM_LIMIT)


def _silu(x):
    return x * jax.nn.sigmoid(x)


def _layer_norm(y, g, b):
    mu = jnp.mean(y, axis=-1, keepdims=True)
    d = y - mu
    var = jnp.mean(d * d, axis=-1, keepdims=True)
    return d * lax.rsqrt(var + LN_EPS) * g + b


def _load_weight_bf16(w_hbm, w_bf, stage, sem, ck):
    n = w_hbm.shape[0] // ck

    def cp(c, slot):
        return pltpu.make_async_copy(w_hbm.at[pl.ds(pl.multiple_of(c * ck, ck), ck)], stage.at[slot], sem.at[slot])

    cp(0, 0).start()

    def body(c, carry):
        slot = c % 2

        @pl.when(c + 1 < n)
        def _():
            cp(c + 1, 1 - slot).start()

        cp(c, slot).wait()
        w_bf[pl.ds(pl.multiple_of(c * ck, ck), ck), :] = stage[slot].astype(BF16)
        return carry

    lax.fori_loop(0, n, body, 0)


def _mod_specs(l, j, d, tm, n_prompt_tiles, tiles_per_batch, nb):
    s = pl.BlockSpec((None, tm, d), lambda i: (l, jnp.maximum(i - n_prompt_tiles, 0), j))
    p = pl.BlockSpec((None, None, 1, d), lambda i: (l, jnp.minimum(i // tiles_per_batch, nb - 1), 0, j))
    return [s, p]


def _adaln_kernel(c_ref, w_ref, b_ref, os_ref, op_ref, *, ns_rows):
    c = c_ref[...]
    s = _silu(c).astype(BF16)
    r = jnp.dot(s, w_ref[...].astype(BF16), preferred_element_type=F32) + b_ref[...]
    os_ref[...] = r[:ns_rows]
    op_ref[...] = r[ns_rows:]


def _adaln(c_all, ada_w, ada_b, ns_rows):
    depth, d, n6 = ada_w.shape
    cm = c_all.shape[0]
    tn = 1024
    return pl.pallas_call(
        functools.partial(_adaln_kernel, ns_rows=ns_rows),
        grid=(depth, n6 // tn),
        in_specs=[pl.BlockSpec((cm, d), lambda l, j: (0, 0)),
                  pl.BlockSpec((None, d, tn), lambda l, j: (l, 0, j)),
                  pl.BlockSpec((None, 1, tn), lambda l, j: (l, 0, j))],
        out_specs=[pl.BlockSpec((None, ns_rows, tn), lambda l, j: (l, 0, j)),
                   pl.BlockSpec((None, cm - ns_rows, tn), lambda l, j: (l, 0, j))],
        out_shape=[jax.ShapeDtypeStruct((depth, ns_rows, n6), F32),
                   jax.ShapeDtypeStruct((depth, cm - ns_rows, n6), F32)],
        compiler_params=_cparams(("arbitrary", "arbitrary")),
        name="adaln",
    )(c_all, ada_w, ada_b.reshape(depth, 1, n6))


def _inproj_kernel(x_ref, scs_ref, scp_ref, shs_ref, shp_ref, w_hbm, *rest, n_prompt_tiles, nc, odd):
    if odd:
        lng_ref, lnb_ref, u_ref, v_ref, w_bf, hb, stage, sem = rest
    else:
        z_ref, w_bf, hb, stage, sem = rest
    i = pl.program_id(0)

    @pl.when(i == 0)
    def _():
        _load_weight_bf16(w_hbm, w_bf, stage, sem, stage.shape[1])

    is_s = i >= n_prompt_tiles
    for r0 in range(0, hb.shape[0], ROW_CHUNK):
        rs = slice(r0, r0 + ROW_CHUNK)
        sc = jnp.where(is_s, scs_ref[rs, :], scp_ref[...])
        sh = jnp.where(is_s, shs_ref[rs, :], shp_ref[...])
        hb[rs, :] = (x_ref[rs, :] * (1.0 + sc) + sh).astype(BF16)
    n = w_bf.shape[1]
    for c in range(n // nc):
        z = jnp.dot(hb[...], w_bf[:, c * nc:(c + 1) * nc], preferred_element_type=F32)
        if not odd:
            z_ref[:, c * nc:(c + 1) * nc] = z
        else:
            z = jax.nn.gelu(z)
            half = n // 2
            if c * nc < half:
                u_ref[:, c * nc:(c + 1) * nc] = z
            else:
                v_ref[:, c * nc - half:(c + 1) * nc - half] = z
    if odd:
        v_ref[...] = _layer_norm(v_ref[...], lng_ref[...], lnb_ref[...])


def _inproj(x_all, mod_s, mod_p, l, w, n_prompt_tiles, tiles_per_batch, nb, ln=None):
    ntp, d = x_all.shape
    n = w.shape[1]
    odd = ln is not None
    ck, nc = 128, 1024
    specs = [pl.BlockSpec((TM, d), lambda i: (i, 0))]
    specs += _mod_specs(l, 1, d, TM, n_prompt_tiles, tiles_per_batch, nb)
    specs += _mod_specs(l, 0, d, TM, n_prompt_tiles, tiles_per_batch, nb)
    specs += [pl.BlockSpec(memory_space=pl.ANY)]
    args = [x_all, mod_s, mod_p, mod_s, mod_p, w]
    if odd:
        specs += [pl.BlockSpec((1, n // 2), lambda i: (0, 0))] * 2
        args += [ln[0].reshape(1, -1), ln[1].reshape(1, -1)]
        out_specs = [pl.BlockSpec((TM, n // 2), lambda i: (i, 0))] * 2
        out_shape = [jax.ShapeDtypeStruct((ntp, n // 2), F32)] * 2
    else:
        out_specs = pl.BlockSpec((TM, n), lambda i: (i, 0))
        out_shape = jax.ShapeDtypeStruct((ntp, n), F32)
    return pl.pallas_call(
        functools.partial(_inproj_kernel, n_prompt_tiles=n_prompt_tiles, nc=nc, odd=odd),
        grid=(ntp // TM,),
        in_specs=specs, out_specs=out_specs, out_shape=out_shape,
        scratch_shapes=[pltpu.VMEM((d, n), BF16), pltpu.VMEM((TM, d), BF16),
                        pltpu.VMEM((2, ck, n), F32), pltpu.SemaphoreType.DMA((2,))],
        compiler_params=_cparams(("arbitrary",)),
        name="inproj_odd" if odd else "inproj_even",
    )(*args)


def _forget_lower_bound(rows, l):
    m = functools.reduce(jnp.maximum, rows)
    e = [jnp.exp(r - m) for r in rows]
    return sum(e[:l + 1]) / sum(e)


def _gla_prompt_kernel(q_ref, f_ref, v_ref, g_ref, lbt_ref, rms_ref, o_ref, s_ref,
                       st, qe_s, ke_s, kd_s, es_s, o_s, *, l, n_heads, dk):
    tb = pl.program_id(1)
    rows = q_ref.shape[0]
    nchunk = rows // GLA_CHUNK

    @pl.when(tb == 0)
    def _():
        st[...] = jnp.zeros_like(st)

    lb = _forget_lower_bound([lbt_ref[r:r + 1, :] for r in range(lbt_ref.shape[0])], l)
    fp = f_ref[...]
    logf = jnp.log(lb + (1.0 - lb) * jax.nn.sigmoid(fp))
    k = (1.0 - lb) * jax.nn.sigmoid(-fp)
    rid = lax.broadcasted_iota(jnp.int32, logf.shape, 0) % GLA_CHUNK
    bcum = logf
    s = 1
    while s < GLA_CHUNK:
        bcum = bcum + jnp.where(rid >= s, pltpu.roll(bcum, s, axis=0), 0.0)
        s *= 2
    b3 = bcum.reshape(nchunk, GLA_CHUNK, bcum.shape[1])
    blast = b3[:, GLA_CHUNK - 1:GLA_CHUNK, :]
    es_s[...] = jnp.exp(blast).reshape(nchunk, bcum.shape[1])
    blast_full = jnp.broadcast_to(blast, b3.shape).reshape(bcum.shape)
    qe_s[...] = q_ref[...] * jnp.exp(bcum)
    ke_s[...] = k * jnp.exp(-bcum)
    kd_s[...] = k * jnp.exp(blast_full - bcum)

    ri = lax.broadcasted_iota(jnp.int32, (GLA_CHUNK, GLA_CHUNK), 0)
    ci = lax.broadcasted_iota(jnp.int32, (GLA_CHUNK, GLA_CHUNK), 1)
    causal = ri >= ci

    def chunk_step(c, carry):
        r0 = pl.multiple_of(c * GLA_CHUNK, GLA_CHUNK)
        es_row = es_s[pl.ds(c, 1), :]
        for h in range(n_heads):
            cols = slice(h * dk, (h + 1) * dk)
            qe = qe_s[pl.ds(r0, GLA_CHUNK), cols]
            ke = ke_s[pl.ds(r0, GLA_CHUNK), cols]
            kd = kd_s[pl.ds(r0, GLA_CHUNK), cols]
            v = v_ref[pl.ds(r0, GLA_CHUNK), cols]
            att = lax.dot_general(qe, ke, (((1,), (1,)), ((), ())), preferred_element_type=F32)
            att = jnp.where(causal, att, 0.0)
            sth = st[h]
            o = jnp.dot(att, v, preferred_element_type=F32)
            o = o + lax.dot_general(qe, sth, (((1,), (1,)), ((), ())), preferred_element_type=F32)
            upd = lax.dot_general(v, kd, (((0,), (0,)), ((), ())), preferred_element_type=F32)
            st[h] = es_row[:, cols] * sth + upd
            o_s[pl.ds(r0, GLA_CHUNK), cols] = o
        return carry

    lax.fori_loop(0, nchunk, chunk_step, 0, unroll=2)

    for h in range(n_heads):
        cols = slice(h * dk, (h + 1) * dk)
        o = o_s[:, cols]
        o = o * lax.rsqrt(jnp.mean(o * o, axis=-1, keepdims=True) + RMS_EPS)
        o_ref[:, cols] = (o * rms_ref[:, cols] * _silu(g_ref[:, cols])).astype(o_ref.dtype)

    @pl.when(tb == pl.num_programs(1) - 1)
    def _():
        for h in range(n_heads):
            s_ref[0, h] = st[h].T


def _gla_prompt(z, lb_table, rms_w, l, nb, t, n_heads, dk):
    d_a = n_heads * dk
    tbk = 512
    nt = t // tbk
    row = lambda b, tb: b * nt + tb
    zspec = lambda sec: pl.BlockSpec((tbk, d_a), lambda b, tb: (row(b, tb), sec))
    return pl.pallas_call(
        functools.partial(_gla_prompt_kernel, l=l, n_heads=n_heads, dk=dk),
        grid=(nb, nt),
        in_specs=[zspec(0), zspec(1), zspec(2), zspec(3),
                  pl.BlockSpec(lb_table.shape, lambda b, tb: (0, 0)),
                  pl.BlockSpec((1, d_a), lambda b, tb: (0, 0))],
        out_specs=[pl.BlockSpec((tbk, d_a), lambda b, tb: (row(b, tb), 0)),
                   pl.BlockSpec((1, n_heads, dk, dk), lambda b, tb: (b, 0, 0, 0))],
        out_shape=[jax.ShapeDtypeStruct((nb * t, d_a), BF16),
                   jax.ShapeDtypeStruct((nb, n_heads, dk, dk), F32)],
        scratch_shapes=[pltpu.VMEM((n_heads, dk, dk), F32),
                        pltpu.VMEM((tbk, d_a), F32), pltpu.VMEM((tbk, d_a), F32), pltpu.VMEM((tbk, d_a), F32),
                        pltpu.VMEM((tbk // GLA_CHUNK, d_a), F32), pltpu.VMEM((tbk, d_a), F32)],
        compiler_params=_cparams(("arbitrary", "arbitrary")),
        name="gla_prompt",
    )(z, z, z, z, lb_table, rms_w.reshape(1, d_a))


def _gla_sample_kernel(qt_ref, ft_ref, lbt_ref, v_ref, g_ref, rms_ref, s0_ref, o_ref, s_ref, *, l):
    ns = s0_ref.shape[0]
    lbc = _forget_lower_bound([lbt_ref[r] for r in range(lbt_ref.shape[0])], l)
    fp = ft_ref[...]
    fcol = lbc + (1.0 - lbc) * jax.nn.sigmoid(fp)
    kcol = (1.0 - lbc) * jax.nn.sigmoid(-fp)
    qcol = qt_ref[...]
    for b in range(ns):
        v = v_ref[b:b + 1, :]
        s_new = fcol[:, b:b + 1] * s0_ref[b, 0] + kcol[:, b:b + 1] * v
        s_ref[b, 0] = s_new
        o_ref[b:b + 1, :] = jnp.sum(qcol[:, b:b + 1] * s_new, axis=0, keepdims=True)
    o = o_ref[...]
    o = o * lax.rsqrt(jnp.mean(o * o, axis=-1, keepdims=True) + RMS_EPS)
    o_ref[...] = o * rms_ref[...] * _silu(g_ref[...])


def _gla_sample(qt, ft, lbt_col, v_s, g_s, rms_w, s0, l):
    ns, n_heads, dk, dv = s0.shape
    return pl.pallas_call(
        functools.partial(_gla_sample_kernel, l=l),
        grid=(n_heads,),
        in_specs=[pl.BlockSpec((None, dk, ns), lambda h: (h, 0, 0)),
                  pl.BlockSpec((None, dk, ns), lambda h: (h, 0, 0)),
                  pl.BlockSpec((lbt_col.shape[0], None, dk, 1), lambda h: (0, h, 0, 0)),
                  pl.BlockSpec((ns, dv), lambda h: (0, h)),
                  pl.BlockSpec((ns, dv), lambda h: (0, h)),
                  pl.BlockSpec((1, dv), lambda h: (0, h)),
                  pl.BlockSpec((ns, 1, dk, dv), lambda h: (0, h, 0, 0))],
        out_specs=[pl.BlockSpec((ns, dv), lambda h: (0, h)),
                   pl.BlockSpec((ns, 1, dk, dv), lambda h: (0, h, 0, 0))],
        out_shape=[jax.ShapeDtypeStruct((ns, n_heads * dv), F32),
                   jax.ShapeDtypeStruct(s0.shape, F32)],
        compiler_params=_cparams(("arbitrary",)),
        name="gla_sample",
    )(qt, ft, lbt_col, v_s, g_s, rms_w.reshape(1, -1), s0)


def _pool_mix(win_sums, u, inv_cnt, pw_ref, ps_ref, o_ref, gb):
    for gi in range(len(POOL_WINDOWS)):
        cols = slice(gi * gb, (gi + 1) * gb)
        pooled = win_sums[gi] * inv_cnt[gi] - u[:, cols]
        mixed = jnp.dot(pooled.astype(BF16), pw_ref[gi].astype(BF16), preferred_element_type=F32)
        o_ref[:, cols] = (mixed * ps_ref[:, cols]).astype(o_ref.dtype)


def _pool_prompt_kernel(u_ref, pw_ref, ps_ref, o_ref, tail_ref, buf, *, hist_pad):
    tb = pl.program_id(1)
    rows, d_b = u_ref.shape
    gb = d_b // len(POOL_WINDOWS)

    @pl.when(tb == 0)
    def _():
        buf[0:hist_pad, :] = jnp.zeros((hist_pad, d_b), F32)

    u = u_ref[...]
    buf[hist_pad:, :] = u
    acc = buf[...]
    sums = []
    span = 1
    for gi, w in enumerate(POOL_WINDOWS):
        while span < w:
            acc = acc + pltpu.roll(acc, span, axis=0)
            span *= 2
        sums.append(acc[hist_pad:, gi * gb:(gi + 1) * gb])
    pos = tb * rows + lax.broadcasted_iota(jnp.int32, (rows, gb), 0)
    inv_cnt = [1.0 / jnp.minimum(pos + 1, w).astype(F32) for w in POOL_WINDOWS]
    _pool_mix(sums, u, inv_cnt, pw_ref, ps_ref, o_ref, gb)
    buf[0:hist_pad, :] = u[rows - hist_pad:, :]
    tail_ref[0] = u[rows - hist_pad:, :]


def _pool_prompt(z, pool_w, pool_scale, nb, t, d_b, u_col_block, hist_pad):
    tbk = 512
    nt = t // tbk
    return pl.pallas_call(
        functools.partial(_pool_prompt_kernel, hist_pad=hist_pad),
        grid=(nb, nt),
        in_specs=[pl.BlockSpec((tbk, d_b), lambda b, tb: (b * nt + tb, u_col_block)),
                  pl.BlockSpec(pool_w.shape, lambda b, tb: (0, 0, 0)),
                  pl.BlockSpec((1, d_b), lambda b, tb: (0, 0))],
        out_specs=[pl.BlockSpec((tbk, d_b), lambda b, tb: (b * nt + tb, 0)),
                   pl.BlockSpec((1, hist_pad, d_b), lambda b, tb: (b, 0, 0))],
        out_shape=[jax.ShapeDtypeStruct((nb * t, d_b), BF16),
                   jax.ShapeDtypeStruct((nb, hist_pad, d_b), F32)],
        scratch_shapes=[pltpu.VMEM((hist_pad + tbk, d_b), F32)],
        compiler_params=_cparams(("arbitrary", "arbitrary")),
        name="pool_prompt",
    )(z, pool_w, pool_scale.reshape(1, d_b))


def _pool_sample_kernel(hist_ref, u_ref, pw_ref, ps_ref, o_ref, nb_ref, *, n_hist):
    ns, d_b = u_ref.shape
    gb = d_b // len(POOL_WINDOWS)
    u = u_ref[...]
    row = lambda j: hist_ref[:, j * d_b:(j + 1) * d_b]
    sums, inv_cnt = [], []
    for gi, w in enumerate(POOL_WINDOWS):
        cols = slice(gi * gb, (gi + 1) * gb)
        acc = u[:, cols]
        for j in range(n_hist - (w - 1), n_hist):
            acc = acc + row(j)[:, cols]
        sums.append(acc)
        inv_cnt.append(1.0 / float(min(PAST_LEN + 1, w)))
    _pool_mix(sums, u, inv_cnt, pw_ref, ps_ref, o_ref, gb)
    nb_ref[:, 0:(n_hist - 1) * d_b] = hist_ref[:, d_b:]
    nb_ref[:, (n_hist - 1) * d_b:] = u


def _pool_sample(hist2d, z, pool_w, pool_scale, ns, d_b, u_row_block, u_col_block, n_hist):
    return pl.pallas_call(
        functools.partial(_pool_sample_kernel, n_hist=n_hist),
        grid=(1,),
        in_specs=[pl.BlockSpec(hist2d.shape, lambda i: (0, 0)),
                  pl.BlockSpec((ns, d_b), lambda i: (u_row_block, u_col_block)),
                  pl.BlockSpec(pool_w.shape, lambda i: (0, 0, 0)),
                  pl.BlockSpec((1, d_b), lambda i: (0, 0))],
        out_specs=[pl.BlockSpec((ns, d_b), lambda i: (0, 0)),
                   pl.BlockSpec(hist2d.shape, lambda i: (0, 0))],
        out_shape=[jax.ShapeDtypeStruct((ns, d_b), BF16),
                   jax.ShapeDtypeStruct(hist2d.shape, F32)],
        compiler_params=_cparams(("arbitrary",)),
        name="pool_sample",
    )(hist2d, z, pool_w, pool_scale.reshape(1, d_b))


def _spatial_kernel(u_ref, v_ref, wsp_ref, bt_ref, w00_ref, b0_ref, o_ref, *, n_prompt_tiles, n_heads):
    i = pl.program_id(0)
    rows, d_c = u_ref.shape
    gc = d_c // n_heads

    @pl.when(i < n_prompt_tiles)
    def _():
        ri = lax.broadcasted_iota(jnp.int32, (C_CHUNK, C_CHUNK), 0)
        ci = lax.broadcasted_iota(jnp.int32, (C_CHUNK, C_CHUNK), 1)
        for h in range(n_heads):
            wc = jnp.where(ri >= ci, wsp_ref[h], 0.0).astype(BF16)
            bias = bt_ref[:, h:h + 1]
            for c in range(rows // C_CHUNK):
                rs = slice(c * C_CHUNK, (c + 1) * C_CHUNK)
                cs = slice(h * gc, (h + 1) * gc)
                mixed = jnp.dot(wc, v_ref[rs, cs].astype(BF16), preferred_element_type=F32) + bias
                o_ref[rs, cs] = (u_ref[rs, cs] * mixed).astype(o_ref.dtype)

    @pl.when(i >= n_prompt_tiles)
    def _():
        o_ref[...] = (u_ref[...] * (v_ref[...] * w00_ref[...] + b0_ref[...])).astype(o_ref.dtype)


def _spatial(u, v, w_sp, b_sp, n_prompt_tiles):
    ntp, d_c = u.shape
    n_heads = w_sp.shape[0]
    gc = d_c // n_heads
    w00 = jnp.repeat(w_sp[:, 0, 0], gc).reshape(1, d_c)
    b0 = jnp.repeat(b_sp[:, 0], gc).reshape(1, d_c)
    tile = pl.BlockSpec((TM, d_c), lambda i: (i, 0))
    return pl.pallas_call(
        functools.partial(_spatial_kernel, n_prompt_tiles=n_prompt_tiles, n_heads=n_heads),
        grid=(ntp // TM,),
        in_specs=[tile, tile,
                  pl.BlockSpec(w_sp.shape, lambda i: (0, 0, 0)),
                  pl.BlockSpec((C_CHUNK, n_heads), lambda i: (0, 0)),
                  pl.BlockSpec((1, d_c), lambda i: (0, 0)),
                  pl.BlockSpec((1, d_c), lambda i: (0, 0))],
        out_specs=tile,
        out_shape=jax.ShapeDtypeStruct((ntp, d_c), BF16),
        compiler_params=_cparams(("arbitrary",)),
        name="spatial",
    )(u, v, w_sp, b_sp.T, w00, b0)


def _outproj_kernel(*refs, n_lhs, n_prompt_tiles, alpha):
    lhs = refs[:n_lhs]
    (x_ref, g1s, g1p, sc2s, sc2p, sh2s, sh2p, lng_ref, lnb_ref, wr_ref, w_hbm,
     x1_ref, h2_ref, lg_ref, h2r_ref, w_bf, stage, sem) = refs[n_lhs:]
    i = pl.program_id(0)

    @pl.when(i == 0)
    def _():
        _load_weight_bf16(w_hbm, w_bf, stage, sem, stage.shape[1])

    is_s = i >= n_prompt_tiles
    out = None
    k0 = 0
    for a_ref in lhs:
        kk = a_ref.shape[1]
        part = jnp.dot(a_ref[...], w_bf[k0:k0 + kk, :], preferred_element_type=F32)
        out = part if out is None else out + part
        k0 += kk
    for r0 in range(0, out.shape[0], ROW_CHUNK):
        rs = slice(r0, r0 + ROW_CHUNK)
        g1 = jnp.where(is_s, g1s[rs, :], g1p[...])
        x1 = _layer_norm(alpha * x_ref[rs, :] + g1 * out[rs], lng_ref[...], lnb_ref[...])
        x1_ref[rs, :] = x1
        sc2 = jnp.where(is_s, sc2s[rs, :], sc2p[...])
        sh2 = jnp.where(is_s, sh2s[rs, :], sh2p[...])
        h2_ref[rs, :] = x1 * (1.0 + sc2) + sh2
    lg_ref[...] = lax.dot_general(wr_ref[...], h2_ref[...], (((1,), (1,)), ((), ())),
                                  precision=lax.Precision.HIGHEST, preferred_element_type=F32)
    pitch = h2r_ref.shape[0] // h2_ref.shape[0]
    rpt = h2_ref.shape[1] // LANES
    for j in range(rpt):
        h2r_ref[pl.ds(j, h2_ref.shape[0], stride=pitch), :] = h2_ref[:, j * LANES:(j + 1) * LANES]
    for j in range(rpt, pitch):
        h2r_ref[pl.ds(j, h2_ref.shape[0], stride=pitch), :] = jnp.zeros((h2_ref.shape[0], LANES), F32)


def _outproj(lhs, x_all, mod_s, mod_p, l, w_out, ln_g, ln_b, w_router_t, n_prompt_tiles, tiles_per_batch, nb, alpha):
    ntp, d = x_all.shape
    n_e = w_router_t.shape[0]
    ck = 256
    tile = pl.BlockSpec((TM, d), lambda i: (i, 0))
    specs = [pl.BlockSpec((TM, a.shape[1]), lambda i: (i, 0)) for a in lhs] + [tile]
    for j in (2, 4, 3):
        specs += _mod_specs(l, j, d, TM, n_prompt_tiles, tiles_per_batch, nb)
    specs += [pl.BlockSpec((1, d), lambda i: (0, 0))] * 2
    specs += [pl.BlockSpec((n_e, d), lambda i: (0, 0)), pl.BlockSpec(memory_space=pl.ANY)]
    return pl.pallas_call(
        functools.partial(_outproj_kernel, n_lhs=len(lhs), n_prompt_tiles=n_prompt_tiles, alpha=alpha),
        grid=(ntp // TM,),
        in_specs=specs,
        out_specs=[tile, tile, pl.BlockSpec((n_e, TM), lambda i: (0, i)),
                   pl.BlockSpec((TM * _pitch(d // LANES), LANES), lambda i: (i, 0))],
        out_shape=[jax.ShapeDtypeStruct((ntp, d), F32), jax.ShapeDtypeStruct((ntp, d), F32),
                   jax.ShapeDtypeStruct((n_e, ntp), F32),
                   jax.ShapeDtypeStruct((ntp * _pitch(d // LANES), LANES), F32)],
        scratch_shapes=[pltpu.VMEM(w_out.shape, BF16), pltpu.VMEM((2, ck, d), F32), pltpu.SemaphoreType.DMA((2,))],
        compiler_params=_cparams(("arbitrary",)),
        name="outproj",
    )(*lhs, x_all, *([mod_s, mod_p] * 3), ln_g.reshape(1, d), ln_b.reshape(1, d), w_router_t, w_out)


def _first_argmax(vals, iota, axis, size):
    m = jnp.max(vals, axis=axis, keepdims=True)
    idx = jnp.min(jnp.where(vals == m, iota, size), axis=axis, keepdims=True)
    return m, idx


def _route_kernel(lg_ref, bias_ref, e_ref, w_ref):
    n_e, tm = lg_ref.shape
    gsz = n_e // N_GROUPS
    neg = -jnp.inf
    scores = jax.nn.sigmoid(lg_ref[...])
    sel = scores + bias_ref[...]
    sel3 = sel.reshape(N_GROUPS, gsz, tm)
    io3 = lax.broadcasted_iota(jnp.int32, sel3.shape, 1)
    m1, i1 = _first_argmax(sel3, io3, 1, gsz)
    m2 = jnp.max(jnp.where(io3 == i1, neg, sel3), axis=1, keepdims=True)
    gscore = (m1 + m2).reshape(N_GROUPS, tm)
    iog = lax.broadcasted_iota(jnp.int32, gscore.shape, 0)
    gsel = jnp.zeros(gscore.shape, F32)
    cur = gscore
    for _ in range(TOPK_GROUPS):
        _, gi = _first_argmax(cur, iog, 0, N_GROUPS)
        pick = iog == gi
        gsel = jnp.where(pick, 1.0, gsel)
        cur = jnp.where(pick, neg, cur)
    esel = jnp.broadcast_to(gsel.reshape(N_GROUPS, 1, tm), sel3.shape).reshape(n_e, tm)
    ioe = lax.broadcasted_iota(jnp.int32, sel.shape, 0)
    iok = lax.broadcasted_iota(jnp.int32, (TOP_K, tm), 0)
    cur = jnp.where(esel > 0.5, sel, neg)
    eidx = jnp.zeros((TOP_K, tm), jnp.int32)
    sc = jnp.zeros((TOP_K, tm), F32)
    for r in range(TOP_K):
        _, ei = _first_argmax(cur, ioe, 0, n_e)
        pick = ioe == ei
        val = jnp.sum(jnp.where(pick, scores, 0.0), axis=0, keepdims=True)
        eidx = jnp.where(iok == r, ei, eidx)
        sc = jnp.where(iok == r, val, sc)
        cur = jnp.where(pick, neg, cur)
    e_ref[...] = eidx
    w_ref[...] = sc / jnp.sum(sc, axis=0, keepdims=True) * ROUTED_SCALE


def _route(logits_t, router_bias):
    n_e, ntp = logits_t.shape
    return pl.pallas_call(
        _route_kernel,
        grid=(ntp // TM,),
        in_specs=[pl.BlockSpec((n_e, TM), lambda i: (0, i)), pl.BlockSpec((n_e, 1), lambda i: (0, 0))],
        out_specs=[pl.BlockSpec((TOP_K, TM), lambda i: (0, i))] * 2,
        out_shape=[jax.ShapeDtypeStruct((TOP_K, ntp), jnp.int32), jax.ShapeDtypeStruct((TOP_K, ntp), F32)],
        compiler_params=_cparams(("arbitrary",)),
        name="route",
    )(logits_t, router_bias.reshape(n_e, 1))


def _dispatch_tables(eidx_t, n_e):
    flat_e = eidx_t.T.reshape(-1)
    nk = flat_e.shape[0]
    n_steps = nk // BM + n_e + 1
    abits = (nk + BM - 1).bit_length()
    experts = jnp.arange(n_e, dtype=jnp.int32)
    counts = jnp.sum((flat_e[:, None] == experts[None, :]).astype(jnp.int32), axis=0)
    nblk = (counts + BM - 1) // BM
    bends = jnp.cumsum(nblk)
    total = bends[-1]
    j = jnp.arange(BM, dtype=jnp.int32)
    real = jnp.left_shift(flat_e, abits) | jnp.arange(nk, dtype=jnp.int32)
    pad = jnp.where(j[None, :] < (nblk * BM - counts)[:, None],
                    jnp.left_shift(experts, abits)[:, None] | (nk + j)[None, :], jnp.iinfo(jnp.int32).max)
    ids = jnp.sort(jnp.concatenate([real, pad.reshape(-1)])) & ((1 << abits) - 1)
    table = jnp.concatenate([nk + j, ids] + [nk + j] * (NSLOT - 1))
    g = jnp.arange(n_steps, dtype=jnp.int32)
    e_of = jnp.sum((bends[None, :] <= g[:, None]).astype(jnp.int32), axis=1)
    e_last = jnp.sum((bends <= total - 1).astype(jnp.int32))
    blk_e = jnp.minimum(jnp.where(g < total, e_of, e_last), n_e - 1)
    return blk_e.astype(jnp.int32), total.reshape(1).astype(jnp.int32), table


def _experts_kernel(blk_e, total_ref, table, h2_hbm, wg_ref, wu_ref, wd_ref, y_hbm,
                    xbuf, ybuf, xb, hb, wg_bf, wu_bf, wd_bf, gsem, ssem, *, ntp, nc):
    g = pl.program_id(0)
    total = total_ref[0]
    s = g % NSLOT
    s1 = (g + 1) % NSLOT
    s2 = (g + 2) % NSLOT
    kbits = TOP_K.bit_length() - 1
    d = xb.shape[1]
    rpt = d // LANES
    pitch = xbuf.shape[1] // BM

    def gather(j, slot, r):
        a = table[(j + 1) * BM + r]
        tok = jnp.minimum(lax.shift_right_logical(a, kbits), ntp - 1)
        return pltpu.make_async_copy(h2_hbm.at[pl.ds(tok * pitch, rpt)], xbuf.at[slot, pl.ds(r * pitch, rpt)],
                                     gsem.at[slot])

    def scatter(j, slot, r):
        a = table[(j + 1) * BM + r]
        return pltpu.make_async_copy(ybuf.at[slot, pl.ds(r * pitch, rpt)],
                                     y_hbm.at[pl.ds(pl.multiple_of(a * rpt, rpt), rpt)], ssem.at[slot])

    def wait_gather(slot):
        for r in range(BM):
            pltpu.make_async_copy(h2_hbm.at[pl.ds(0, rpt)], xbuf.at[slot, pl.ds(r * pitch, rpt)], gsem.at[slot]).wait()

    def wait_scatter(slot):
        for r in range(BM):
            pltpu.make_async_copy(ybuf.at[slot, pl.ds(r * pitch, rpt)], y_hbm.at[pl.ds(0, rpt)], ssem.at[slot]).wait()

    @pl.when(g == 0)
    def _():
        ybuf[NSLOT - 1] = jnp.zeros(ybuf.shape[1:], F32)

        def body(r, c):
            gather(0, 0, r).start()
            gather(1, 1, r).start(priority=1)
            return c
        lax.fori_loop(0, BM, body, 0, unroll=8)

    @pl.when(g < total)
    def _():
        prev = blk_e[jnp.maximum(g - 1, 0)]

        @pl.when(jnp.logical_or(g == 0, blk_e[g] != prev))
        def _():
            wg_bf[...] = wg_ref[...].astype(BF16)
            wu_bf[...] = wu_ref[...].astype(BF16)
            wd_bf[...] = wd_ref[...].astype(BF16)

        wait_gather(s)

        @pl.when(g >= 2)
        def _():
            wait_scatter(s)

        d_e = wd_bf.shape[0]
        pieces = d_e // nc + d // nc
        rows = [list(range(p * BM // pieces, (p + 1) * BM // pieces)) for p in range(pieces)]

        def issue(p):
            for r in rows[p]:
                gather(g + 2, s2, r).start(priority=r % 2)
                scatter(g - 1, s2, r).start(priority=(r + 1) % 2)

        for c in range(rpt):
            xb[:, c * LANES:(c + 1) * LANES] = xbuf[s, pl.ds(c, BM, stride=pitch), :].astype(BF16)
        p = 0
        for c in range(d_e // nc):
            cols = slice(c * nc, (c + 1) * nc)
            hg = jnp.dot(xb[...], wg_bf[:, cols], preferred_element_type=F32)
            hu = jnp.dot(xb[...], wu_bf[:, cols], preferred_element_type=F32)
            hb[:, cols] = (_silu(hg) * hu).astype(BF16)
            issue(p)
            p += 1
        for c in range(d // nc):
            y = jnp.dot(hb[...], wd_bf[:, c * nc:(c + 1) * nc], preferred_element_type=F32)
            for cc in range(nc // LANES):
                ybuf[s, pl.ds(c * (nc // LANES) + cc, BM, stride=pitch), :] = y[:, cc * LANES:(cc + 1) * LANES]
            issue(p)
            p += 1

    @pl.when(g == total)
    def _():
        wait_gather(s)
        wait_gather(s1)

        @pl.when(g >= 2)
        def _():
            wait_scatter(s)
        wait_scatter(s1)

        def body(r, c):
            scatter(g - 1, s2, r).start()
            return c
        lax.fori_loop(0, BM, body, 0, unroll=8)
        wait_scatter(s2)


def _experts(h2r, tables, w_gate, w_up, w_down, l):
    d, d_e = w_gate.shape[-2:]
    rpt = d // LANES
    pitch = _pitch(rpt)
    ntp = h2r.shape[0] // pitch
    n_steps = tables[0].shape[0]
    wmap = lambda g, be, *_: (l, be[g], 0, 0)
    return pl.pallas_call(
        functools.partial(_experts_kernel, ntp=ntp, nc=256),
        grid_spec=pltpu.PrefetchScalarGridSpec(
            num_scalar_prefetch=3, grid=(n_steps,),
            in_specs=[pl.BlockSpec(memory_space=pl.ANY),
                      pl.BlockSpec((None, None, d, d_e), wmap),
                      pl.BlockSpec((None, None, d, d_e), wmap),
                      pl.BlockSpec((None, None, d_e, d), wmap)],
            out_specs=pl.BlockSpec(memory_space=pl.ANY),
            scratch_shapes=[pltpu.VMEM((NSLOT, BM * pitch, LANES), F32), pltpu.VMEM((NSLOT, BM * pitch, LANES), F32),
                            pltpu.VMEM((BM, d), BF16), pltpu.VMEM((BM, d_e), BF16),
                            pltpu.VMEM((d, d_e), BF16), pltpu.VMEM((d, d_e), BF16), pltpu.VMEM((d_e, d), BF16),
                            pltpu.SemaphoreType.DMA((NSLOT,)), pltpu.SemaphoreType.DMA((NSLOT,))]),
        out_shape=jax.ShapeDtypeStruct(((ntp * TOP_K + BM) * rpt, LANES), F32),
        compiler_params=_cparams(("arbitrary",)),
        name="experts",
    )(*tables, h2r, w_gate, w_up, w_down)


def _final_kernel(*refs, l, n_prompt_tiles, alpha, split):
    (y_hbm, w_ref, x1_ref, h2_ref, g2s, g2p, lng_ref, lnb_ref, wg_hbm, wu_hbm, wd_hbm) = refs[:11]
    if split:
        yp_ref, ysm_ref = refs[11:13]
        wg_bf, wu_bf, wd_bf, stage_a, stage_b, sem, ybuf, ysem, ffn_s, x2_s = refs[13:]
    else:
        x2_ref = refs[11]
        wg_bf, wu_bf, wd_bf, stage_a, stage_b, sem, ybuf, ysem, ffn_s = refs[12:]
    i = pl.program_id(0)
    tm, d = x1_ref.shape
    rpt = d // LANES
    rec = TOP_K * rpt
    pitch = ybuf.shape[1] // tm
    slot = i % 2

    def fetch(tile, sl, t):
        src = y_hbm.at[pl.ds(pl.multiple_of((tile * tm + t) * rec, rec), rec)]
        return pltpu.make_async_copy(src, ybuf.at[sl, pl.ds(t * pitch, rec)], ysem.at[sl])

    @pl.when(i == 0)
    def _():
        for t in range(tm):
            fetch(0, 0, t).start()
        _load_weight_bf16(wg_hbm.at[l], wg_bf, stage_a, sem, stage_a.shape[1])
        _load_weight_bf16(wu_hbm.at[l], wu_bf, stage_a, sem, stage_a.shape[1])
        _load_weight_bf16(wd_hbm.at[l], wd_bf, stage_b, sem, stage_b.shape[1])

    @pl.when(i + 1 < pl.num_programs(0))
    def _():
        for t in range(tm):
            fetch(i + 1, 1 - slot, t).start()

    hb = h2_ref[...].astype(BF16)
    sg = jnp.dot(hb, wg_bf[...], preferred_element_type=F32)
    su = jnp.dot(hb, wu_bf[...], preferred_element_type=F32)
    shared = jnp.dot((_silu(sg) * su).astype(BF16), wd_bf[...], preferred_element_type=F32)
    is_s = i >= n_prompt_tiles
    dst = x2_s if split else x2_ref

    for t in range(tm):
        fetch(0, slot, t).wait()
    w = w_ref[...]
    wk = [jnp.broadcast_to(w[:, k:k + 1], (tm, LANES)) for k in range(TOP_K)]
    for c in range(rpt):
        acc = shared[:, c * LANES:(c + 1) * LANES]
        for k in range(TOP_K):
            acc = acc + wk[k] * ybuf[slot, pl.ds(k * rpt + c, tm, stride=pitch), :]
        ffn_s[:, c * LANES:(c + 1) * LANES] = acc

    for r0 in range(0, tm, ROW_CHUNK):
        rs = slice(r0, r0 + ROW_CHUNK)
        g2 = jnp.where(is_s, g2s[rs, :], g2p[...])
        dst[rs, :] = _layer_norm(alpha * x1_ref[rs, :] + g2 * ffn_s[rs, :], lng_ref[...], lnb_ref[...])

    if split:
        @pl.when(i < n_prompt_tiles)
        def _():
            yp_ref[...] = x2_s[...]

        @pl.when(i == n_prompt_tiles)
        def _():
            ysm_ref[...] = x2_s[...]


def _final(yslots, wsel, x1, h2, mod_s, mod_p, l, ln_g, ln_b, ws_gate, ws_up, ws_down,
           n_prompt_rows, rows_per_batch, nb, alpha, split):
    ntp, d = x1.shape
    tm = TM_FINAL
    d_s = ws_gate.shape[-1]
    npt = n_prompt_rows // tm
    tpb = rows_per_batch // tm
    nt = ntp // tm
    tile = pl.BlockSpec((tm, d), lambda i: (i, 0))
    specs = [pl.BlockSpec(memory_space=pl.ANY), pl.BlockSpec((tm, TOP_K), lambda i: (i, 0)), tile, tile]
    specs += _mod_specs(l, 5, d, tm, npt, tpb, nb)
    specs += [pl.BlockSpec((1, d), lambda i: (0, 0))] * 2
    specs += [pl.BlockSpec(memory_space=pl.ANY)] * 3
    if split:
        out_specs = [pl.BlockSpec((tm, d), lambda i: (jnp.minimum(i, npt - 1), 0)),
                     pl.BlockSpec((tm, d), lambda i: (0, 0))]
        out_shape = [jax.ShapeDtypeStruct((n_prompt_rows, d), F32), jax.ShapeDtypeStruct((tm, d), F32)]
    else:
        out_specs = tile
        out_shape = jax.ShapeDtypeStruct((ntp, d), F32)
    ck = 256
    scratch = [pltpu.VMEM((d, d_s), BF16), pltpu.VMEM((d, d_s), BF16), pltpu.VMEM((d_s, d), BF16),
               pltpu.VMEM((2, ck, d_s), F32), pltpu.VMEM((2, ck, d), F32), pltpu.SemaphoreType.DMA((2,)),
               pltpu.VMEM((2, tm * _pitch(TOP_K * d // LANES), LANES), F32), pltpu.SemaphoreType.DMA((2,)),
               pltpu.VMEM((tm, d), F32)]
    if split:
        scratch.append(pltpu.VMEM((tm, d), F32))
    return pl.pallas_call(
        functools.partial(_final_kernel, l=l, n_prompt_tiles=npt, alpha=alpha, split=split),
        grid=(nt,),
        in_specs=specs, out_specs=out_specs, out_shape=out_shape,
        scratch_shapes=scratch,
        compiler_params=_cparams(("arbitrary",)),
        name="moe_final",
    )(yslots, wsel, x1, h2, mod_s, mod_p, ln_g.reshape(1, d), ln_b.reshape(1, d),
      ws_gate, ws_up, ws_down)


def kernel(x_prompt, x_sample, state_hgrn, state_pool, c_prompt, c_sample, mix0_w_in, mix0_w_out, hgrn_lb_table, hgrn_rms_w, pool_w, pool_scale, mix1_w_in, gmlp_ln_g, gmlp_ln_b, gmlp_w_sp, gmlp_b_sp, mix1_w_out, ada_w, ada_b, ln_mix_g, ln_mix_b, ln_ffn_g, ln_ffn_b, moe_w_router, moe_router_bias, moe_w_gate, moe_w_up, moe_w_down, shared_w_gate, shared_w_up, shared_w_down):
    nb, t, d = x_prompt.shape
    ns = x_sample.shape[0]
    depth = ada_w.shape[0]
    n_heads, dk = state_hgrn.shape[2], state_hgrn.shape[3]
    d_a = n_heads * dk
    d_b = state_pool.shape[3]
    n_hist = state_pool.shape[2]
    hist_pad = 16
    n_e = moe_w_router.shape[2]
    alpha = (2.0 * depth) ** 0.25
    n_p = nb * t
    npt = n_p // TM
    tpb = t // TM
    assert t % 512 == 0 and ns <= TM and ns % 8 == 0 and n_hist < hist_pad and d_a % d_b == 0

    x_all = jnp.concatenate([x_prompt.reshape(n_p, d), x_sample.reshape(ns, d), jnp.zeros((TM - ns, d), F32)], axis=0)
    c_all = jnp.concatenate([c_sample, jnp.zeros((TM - ns, d), F32), c_prompt, jnp.zeros((8 - nb % 8, d), F32)], axis=0)
    mod_s, mod_p = _adaln(c_all, ada_w, ada_b, TM)
    mod_p = mod_p.reshape(depth, mod_p.shape[1], 1, N_MOD * d)

    hgrn_p, pool_p, hgrn_s, pool_s, chunk_v = [], [], [], [], []
    x = x_all
    for l in range(depth):
        if l % 2 == 0:
            e = l // 2
            z = _inproj(x, mod_s, mod_p, l, mix0_w_in[e], npt, tpb, nb)
            o_a_p, s_p = _gla_prompt(z, hgrn_lb_table, hgrn_rms_w[e], l, nb, t, n_heads, dk)
            o_b_p, tail = _pool_prompt(z, pool_w[e], pool_scale[e], nb, t, d_b, 4 * d_a // d_b, hist_pad)
            hgrn_p.append(s_p)
            pool_p.append(tail[:, hist_pad - n_hist:])
            zs = z[n_p:n_p + ns]
            qt = zs[:, 0:d_a].reshape(ns, n_heads, dk).transpose(1, 2, 0)
            ft = zs[:, d_a:2 * d_a].reshape(ns, n_heads, dk).transpose(1, 2, 0)
            lbt_col = hgrn_lb_table.reshape(-1, n_heads, dk, 1)
            o_a_s, s_s = _gla_sample(qt, ft, lbt_col, zs[:, 2 * d_a:3 * d_a], zs[:, 3 * d_a:4 * d_a],
                                     hgrn_rms_w[e], state_hgrn[e], l)
            o_b_s, buf_s = _pool_sample(state_pool[e].reshape(ns, n_hist * d_b), z, pool_w[e], pool_scale[e],
                                        ns, d_b, n_p // ns, 4 * d_a // d_b, n_hist)
            hgrn_s.append(s_s)
            pool_s.append(buf_s.reshape(ns, n_hist, d_b))
            pad = jnp.zeros((TM - ns, d_a), BF16)
            o_a = jnp.concatenate([o_a_p, o_a_s.astype(BF16), pad], axis=0)
            o_b = jnp.concatenate([o_b_p, o_b_s, jnp.zeros((TM - ns, d_b), BF16)], axis=0)
            lhs, w_out = [o_a, o_b], mix0_w_out[e]
        else:
            o = l // 2
            u, v = _inproj(x, mod_s, mod_p, l, mix1_w_in[o], npt, tpb, nb, ln=(gmlp_ln_g[o], gmlp_ln_b[o]))
            chunk_v.append(v[n_p:n_p + ns].reshape(ns, 1, -1))
            lhs, w_out = [_spatial(u, v, gmlp_w_sp[o], gmlp_b_sp[o], npt)], mix1_w_out[o]
        x1, h2, logits_t, h2r = _outproj(lhs, x, mod_s, mod_p, l, w_out, ln_mix_g[l], ln_mix_b[l],
                                    moe_w_router[l].T, npt, tpb, nb, alpha)
        eidx_t, wsel_t = _route(logits_t, moe_router_bias[l])
        yslots = _experts(h2r, _dispatch_tables(eidx_t, n_e), moe_w_gate, moe_w_up, moe_w_down, l)
        last = l == depth - 1
        res = _final(yslots, wsel_t.T, x1, h2, mod_s, mod_p, l, ln_ffn_g[l], ln_ffn_b[l],
                     shared_w_gate, shared_w_up, shared_w_down, n_p, t, nb, alpha, split=last)
        if last:
            y_p, y_s = res
        else:
            x = res
    return (y_p.reshape(nb, t, d), y_s[:ns].reshape(ns, 1, d),
            jnp.stack(hgrn_p), jnp.stack(pool_p), jnp.stack(hgrn_s), jnp.stack(pool_s), jnp.stack(chunk_v))
```

```python
import functools

import jax
import jax.numpy as jnp
from jax import lax
from jax.experimental import pallas as pl
from jax.experimental.pallas import tpu as pltpu

F32 = jnp.float32
BF16 = jnp.bfloat16

PAST_LEN = 16384
GLA_CHUNK = 32
POOL_WINDOWS = (2, 4, 8, 16)
C_CHUNK = 128
TOP_K = 8
N_GROUPS = 8
TOPK_GROUPS = 4
ROUTED_SCALE = 2.5
LN_EPS = 1e-5
RMS_EPS = 1e-6
N_MOD = 6

TM = 256
TM_FINAL = 128
BM = 256
NSLOT = 3
ROW_CHUNK = 32
LANES = 128


def _pitch(rows):
    assert rows % 8 == 0
    return rows + 4
VMEM_LIMIT = 56 * 1024 * 1024


def _cparams(sem):
    return pltpu.CompilerParams(dimension_semantics=sem, vmem_limit_bytes=VMEM_LIMIT)


def _silu(x):
    return x * jax.nn.sigmoid(x)


def _layer_norm(y, g, b):
    mu = jnp.mean(y, axis=-1, keepdims=True)
    d = y - mu
    var = jnp.mean(d * d, axis=-1, keepdims=True)
    return d * lax.rsqrt(var + LN_EPS) * g + b


def _load_weight_bf16(w_hbm, w_bf, stage, sem, ck):
    n = w_hbm.shape[0] // ck

    def cp(c, slot):
        return pltpu.make_async_copy(w_hbm.at[pl.ds(pl.multiple_of(c * ck, ck), ck)], stage.at[slot], sem.at[slot])

    cp(0, 0).start()

    def body(c, carry):
        slot = c % 2

        @pl.when(c + 1 < n)
        def _():
            cp(c + 1, 1 - slot).start()

        cp(c, slot).wait()
        w_bf[pl.ds(pl.multiple_of(c * ck, ck), ck), :] = stage[slot].astype(BF16)
        return carry

    lax.fori_loop(0, n, body, 0)


def _mod_specs(l, j, d, tm, n_prompt_tiles, tiles_per_batch, nb, lag=0):
    tile = lambda i: jnp.maximum(i - lag, 0)
    s = pl.BlockSpec((None, tm, d), lambda i: (l, jnp.maximum(tile(i) - n_prompt_tiles, 0), j))
    p = pl.BlockSpec((None, None, 1, d), lambda i: (l, jnp.minimum(tile(i) // tiles_per_batch, nb - 1), 0, j))
    return [s, p]


def _adaln_kernel(c_ref, w_ref, b_ref, os_ref, op_ref, *, ns_rows):
    c = c_ref[...]
    s = _silu(c).astype(BF16)
    r = jnp.dot(s, w_ref[...].astype(BF16), preferred_element_type=F32) + b_ref[...]
    os_ref[...] = r[:ns_rows]
    op_ref[...] = r[ns_rows:]


def _adaln(c_all, ada_w, ada_b, ns_rows):
    depth, d, n6 = ada_w.shape
    cm = c_all.shape[0]
    tn = 1024
    return pl.pallas_call(
        functools.partial(_adaln_kernel, ns_rows=ns_rows),
        grid=(depth, n6 // tn),
        in_specs=[pl.BlockSpec((cm, d), lambda l, j: (0, 0)),
                  pl.BlockSpec((None, d, tn), lambda l, j: (l, 0, j)),
                  pl.BlockSpec((None, 1, tn), lambda l, j: (l, 0, j))],
        out_specs=[pl.BlockSpec((None, ns_rows, tn), lambda l, j: (l, 0, j)),
                   pl.BlockSpec((None, cm - ns_rows, tn), lambda l, j: (l, 0, j))],
        out_shape=[jax.ShapeDtypeStruct((depth, ns_rows, n6), F32),
                   jax.ShapeDtypeStruct((depth, cm - ns_rows, n6), F32)],
        compiler_params=_cparams(("arbitrary", "arbitrary")),
        name="adaln",
    )(c_all, ada_w, ada_b.reshape(depth, 1, n6))


def _inproj_kernel(x_ref, scs_ref, scp_ref, shs_ref, shp_ref, w_hbm, *rest, n_prompt_tiles, nc, odd):
    if odd:
        lng_ref, lnb_ref, u_ref, v_ref, w_bf, hb, stage, sem = rest
    else:
        z_ref, w_bf, hb, stage, sem = rest
    i = pl.program_id(0)

    @pl.when(i == 0)
    def _():
        _load_weight_bf16(w_hbm, w_bf, stage, sem, stage.shape[1])

    is_s = i >= n_prompt_tiles
    for r0 in range(0, hb.shape[0], ROW_CHUNK):
        rs = slice(r0, r0 + ROW_CHUNK)
        sc = jnp.where(is_s, scs_ref[rs, :], scp_ref[...])
        sh = jnp.where(is_s, shs_ref[rs, :], shp_ref[...])
        hb[rs, :] = (x_ref[rs, :] * (1.0 + sc) + sh).astype(BF16)
    n = w_bf.shape[1]
    for c in range(n // nc):
        z = jnp.dot(hb[...], w_bf[:, c * nc:(c + 1) * nc], preferred_element_type=F32)
        if not odd:
            z_ref[:, c * nc:(c + 1) * nc] = z
        else:
            z = jax.nn.gelu(z)
            half = n // 2
            if c * nc < half:
                u_ref[:, c * nc:(c + 1) * nc] = z
            else:
                v_ref[:, c * nc - half:(c + 1) * nc - half] = z
    if odd:
        v_ref[...] = _layer_norm(v_ref[...], lng_ref[...], lnb_ref[...])


def _inproj(x_all, mod_s, mod_p, l, w, n_prompt_tiles, tiles_per_batch, nb, ln=None):
    ntp, d = x_all.shape
    n = w.shape[1]
    odd = ln is not None
    ck, nc = 128, 1024
    specs = [pl.BlockSpec((TM, d), lambda i: (i, 0))]
    specs += _mod_specs(l, 1, d, TM, n_prompt_tiles, tiles_per_batch, nb)
    specs += _mod_specs(l, 0, d, TM, n_prompt_tiles, tiles_per_batch, nb)
    specs += [pl.BlockSpec(memory_space=pl.ANY)]
    args = [x_all, mod_s, mod_p, mod_s, mod_p, w]
    if odd:
        specs += [pl.BlockSpec((1, n // 2), lambda i: (0, 0))] * 2
        args += [ln[0].reshape(1, -1), ln[1].reshape(1, -1)]
        out_specs = [pl.BlockSpec((TM, n // 2), lambda i: (i, 0))] * 2
        out_shape = [jax.ShapeDtypeStruct((ntp, n // 2), F32)] * 2
    else:
        out_specs = pl.BlockSpec((TM, n), lambda i: (i, 0))
        out_shape = jax.ShapeDtypeStruct((ntp, n), F32)
    return pl.pallas_call(
        functools.partial(_inproj_kernel, n_prompt_tiles=n_prompt_tiles, nc=nc, odd=odd),
        grid=(ntp // TM,),
        in_specs=specs, out_specs=out_specs, out_shape=out_shape,
        scratch_shapes=[pltpu.VMEM((d, n), BF16), pltpu.VMEM((TM, d), BF16),
                        pltpu.VMEM((2, ck, n), F32), pltpu.SemaphoreType.DMA((2,))],
        compiler_params=_cparams(("arbitrary",)),
        name="inproj_odd" if odd else "inproj_even",
    )(*args)


def _forget_lower_bound(rows, l):
    m = functools.reduce(jnp.maximum, rows)
    e = [jnp.exp(r - m) for r in rows]
    return sum(e[:l + 1]) / sum(e)


def _gla_prompt_kernel(q_ref, f_ref, v_ref, g_ref, lbt_ref, rms_ref, o_ref, s_ref,
                       st, qe_s, ke_s, kd_s, es_s, o_s, *, l, n_heads, dk):
    tb = pl.program_id(1)
    rows = q_ref.shape[0]
    nchunk = rows // GLA_CHUNK

    @pl.when(tb == 0)
    def _():
        st[...] = jnp.zeros_like(st)

    lb = _forget_lower_bound([lbt_ref[r:r + 1, :] for r in range(lbt_ref.shape[0])], l)
    fp = f_ref[...]
    logf = jnp.log(lb + (1.0 - lb) * jax.nn.sigmoid(fp))
    k = (1.0 - lb) * jax.nn.sigmoid(-fp)
    rid = lax.broadcasted_iota(jnp.int32, logf.shape, 0) % GLA_CHUNK
    bcum = logf
    s = 1
    while s < GLA_CHUNK:
        bcum = bcum + jnp.where(rid >= s, pltpu.roll(bcum, s, axis=0), 0.0)
        s *= 2
    b3 = bcum.reshape(nchunk, GLA_CHUNK, bcum.shape[1])
    blast = b3[:, GLA_CHUNK - 1:GLA_CHUNK, :]
    es_s[...] = jnp.exp(blast).reshape(nchunk, bcum.shape[1])
    blast_full = jnp.broadcast_to(blast, b3.shape).reshape(bcum.shape)
    qe_s[...] = q_ref[...] * jnp.exp(bcum)
    ke_s[...] = k * jnp.exp(-bcum)
    kd_s[...] = k * jnp.exp(blast_full - bcum)

    ri = lax.broadcasted_iota(jnp.int32, (GLA_CHUNK, GLA_CHUNK), 0)
    ci = lax.broadcasted_iota(jnp.int32, (GLA_CHUNK, GLA_CHUNK), 1)
    causal = ri >= ci

    def chunk_step(c, carry):
        r0 = pl.multiple_of(c * GLA_CHUNK, GLA_CHUNK)
        es_row = es_s[pl.ds(c, 1), :]
        for h in range(n_heads):
            cols = slice(h * dk, (h + 1) * dk)
            qe = qe_s[pl.ds(r0, GLA_CHUNK), cols]
            ke = ke_s[pl.ds(r0, GLA_CHUNK), cols]
            kd = kd_s[pl.ds(r0, GLA_CHUNK), cols]
            v = v_ref[pl.ds(r0, GLA_CHUNK), cols]
            att = lax.dot_general(qe, ke, (((1,), (1,)), ((), ())), preferred_element_type=F32)
            att = jnp.where(causal, att, 0.0)
            sth = st[h]
            o = jnp.dot(att, v, preferred_element_type=F32)
            o = o + lax.dot_general(qe, sth, (((1,), (1,)), ((), ())), preferred_element_type=F32)
            upd = lax.dot_general(v, kd, (((0,), (0,)), ((), ())), preferred_element_type=F32)
            st[h] = es_row[:, cols] * sth + upd
            o_s[pl.ds(r0, GLA_CHUNK), cols] = o
        return carry

    lax.fori_loop(0, nchunk, chunk_step, 0, unroll=4)

    for h in range(n_heads):
        cols = slice(h * dk, (h + 1) * dk)
        o = o_s[:, cols]
        o = o * lax.rsqrt(jnp.mean(o * o, axis=-1, keepdims=True) + RMS_EPS)
        o_ref[:, cols] = (o * rms_ref[:, cols] * _silu(g_ref[:, cols])).astype(o_ref.dtype)

    @pl.when(tb == pl.num_programs(1) - 1)
    def _():
        for h in range(n_heads):
            s_ref[0, h] = st[h].T


def _gla_prompt(z, lb_table, rms_w, l, nb, t, n_heads, dk):
    d_a = n_heads * dk
    tbk = 512
    nt = t // tbk
    row = lambda b, tb: b * nt + tb
    zspec = lambda sec: pl.BlockSpec((tbk, d_a), lambda b, tb: (row(b, tb), sec))
    return pl.pallas_call(
        functools.partial(_gla_prompt_kernel, l=l, n_heads=n_heads, dk=dk),
        grid=(nb, nt),
        in_specs=[zspec(0), zspec(1), zspec(2), zspec(3),
                  pl.BlockSpec(lb_table.shape, lambda b, tb: (0, 0)),
                  pl.BlockSpec((1, d_a), lambda b, tb: (0, 0))],
        out_specs=[pl.BlockSpec((tbk, d_a), lambda b, tb: (row(b, tb), 0)),
                   pl.BlockSpec((1, n_heads, dk, dk), lambda b, tb: (b, 0, 0, 0))],
        out_shape=[jax.ShapeDtypeStruct((nb * t, d_a), BF16),
                   jax.ShapeDtypeStruct((nb, n_heads, dk, dk), F32)],
        scratch_shapes=[pltpu.VMEM((n_heads, dk, dk), F32),
                        pltpu.VMEM((tbk, d_a), F32), pltpu.VMEM((tbk, d_a), F32), pltpu.VMEM((tbk, d_a), F32),
                        pltpu.VMEM((tbk // GLA_CHUNK, d_a), F32), pltpu.VMEM((tbk, d_a), F32)],
        compiler_params=_cparams(("arbitrary", "arbitrary")),
        name="gla_prompt",
    )(z, z, z, z, lb_table, rms_w.reshape(1, d_a))


def _gla_sample_kernel(qt_ref, ft_ref, lbt_ref, v_ref, g_ref, rms_ref, s0_ref, o_ref, s_ref, *, l):
    ns = s0_ref.shape[0]
    lbc = _forget_lower_bound([lbt_ref[r] for r in range(lbt_ref.shape[0])], l)
    fp = ft_ref[...]
    fcol = lbc + (1.0 - lbc) * jax.nn.sigmoid(fp)
    kcol = (1.0 - lbc) * jax.nn.sigmoid(-fp)
    qcol = qt_ref[...]
    for b in range(ns):
        v = v_ref[b:b + 1, :]
        s_new = fcol[:, b:b + 1] * s0_ref[b, 0] + kcol[:, b:b + 1] * v
        s_ref[b, 0] = s_new
        o_ref[b:b + 1, :] = jnp.sum(qcol[:, b:b + 1] * s_new, axis=0, keepdims=True)
    o = o_ref[...]
    o = o * lax.rsqrt(jnp.mean(o * o, axis=-1, keepdims=True) + RMS_EPS)
    o_ref[...] = o * rms_ref[...] * _silu(g_ref[...])


def _gla_sample(qt, ft, lbt_col, v_s, g_s, rms_w, s0, l):
    ns, n_heads, dk, dv = s0.shape
    return pl.pallas_call(
        functools.partial(_gla_sample_kernel, l=l),
        grid=(n_heads,),
        in_specs=[pl.BlockSpec((None, dk, ns), lambda h: (h, 0, 0)),
                  pl.BlockSpec((None, dk, ns), lambda h: (h, 0, 0)),
                  pl.BlockSpec((lbt_col.shape[0], None, dk, 1), lambda h: (0, h, 0, 0)),
                  pl.BlockSpec((ns, dv), lambda h: (0, h)),
                  pl.BlockSpec((ns, dv), lambda h: (0, h)),
                  pl.BlockSpec((1, dv), lambda h: (0, h)),
                  pl.BlockSpec((ns, 1, dk, dv), lambda h: (0, h, 0, 0))],
        out_specs=[pl.BlockSpec((ns, dv), lambda h: (0, h)),
                   pl.BlockSpec((ns, 1, dk, dv), lambda h: (0, h, 0, 0))],
        out_shape=[jax.ShapeDtypeStruct((ns, n_heads * dv), F32),
                   jax.ShapeDtypeStruct(s0.shape, F32)],
        compiler_params=_cparams(("arbitrary",)),
        name="gla_sample",
    )(qt, ft, lbt_col, v_s, g_s, rms_w.reshape(1, -1), s0)


def _pool_mix(win_sums, u, inv_cnt, pw_ref, ps_ref, o_ref, gb):
    for gi in range(len(POOL_WINDOWS)):
        cols = slice(gi * gb, (gi + 1) * gb)
        pooled = win_sums[gi] * inv_cnt[gi] - u[:, cols]
        mixed = jnp.dot(pooled.astype(BF16), pw_ref[gi].astype(BF16), preferred_element_type=F32)
        o_ref[:, cols] = (mixed * ps_ref[:, cols]).astype(o_ref.dtype)


def _pool_prompt_kernel(u_ref, pw_ref, ps_ref, o_ref, tail_ref, buf, *, hist_pad):
    tb = pl.program_id(1)
    rows, d_b = u_ref.shape
    gb = d_b // len(POOL_WINDOWS)

    @pl.when(tb == 0)
    def _():
        buf[0:hist_pad, :] = jnp.zeros((hist_pad, d_b), F32)

    u = u_ref[...]
    buf[hist_pad:, :] = u
    acc = buf[...]
    sums = []
    span = 1
    for gi, w in enumerate(POOL_WINDOWS):
        while span < w:
            acc = acc + pltpu.roll(acc, span, axis=0)
            span *= 2
        sums.append(acc[hist_pad:, gi * gb:(gi + 1) * gb])
    pos = tb * rows + lax.broadcasted_iota(jnp.int32, (rows, gb), 0)
    inv_cnt = [1.0 / jnp.minimum(pos + 1, w).astype(F32) for w in POOL_WINDOWS]
    _pool_mix(sums, u, inv_cnt, pw_ref, ps_ref, o_ref, gb)
    buf[0:hist_pad, :] = u[rows - hist_pad:, :]
    tail_ref[0] = u[rows - hist_pad:, :]


def _pool_prompt(z, pool_w, pool_scale, nb, t, d_b, u_col_block, hist_pad):
    tbk = 512
    nt = t // tbk
    return pl.pallas_call(
        functools.partial(_pool_prompt_kernel, hist_pad=hist_pad),
        grid=(nb, nt),
        in_specs=[pl.BlockSpec((tbk, d_b), lambda b, tb: (b * nt + tb, u_col_block)),
                  pl.BlockSpec(pool_w.shape, lambda b, tb: (0, 0, 0)),
                  pl.BlockSpec((1, d_b), lambda b, tb: (0, 0))],
        out_specs=[pl.BlockSpec((tbk, d_b), lambda b, tb: (b * nt + tb, 0)),
                   pl.BlockSpec((1, hist_pad, d_b), lambda b, tb: (b, 0, 0))],
        out_shape=[jax.ShapeDtypeStruct((nb * t, d_b), BF16),
                   jax.ShapeDtypeStruct((nb, hist_pad, d_b), F32)],
        scratch_shapes=[pltpu.VMEM((hist_pad + tbk, d_b), F32)],
        compiler_params=_cparams(("arbitrary", "arbitrary")),
        name="pool_prompt",
    )(z, pool_w, pool_scale.reshape(1, d_b))


def _pool_sample_kernel(hist_ref, u_ref, pw_ref, ps_ref, o_ref, nb_ref, *, n_hist):
    ns, d_b = u_ref.shape
    gb = d_b // len(POOL_WINDOWS)
    u = u_ref[...]
    row = lambda j: hist_ref[:, j * d_b:(j + 1) * d_b]
    sums, inv_cnt = [], []
    for gi, w in enumerate(POOL_WINDOWS):
        cols = slice(gi * gb, (gi + 1) * gb)
        acc = u[:, cols]
        for j in range(n_hist - (w - 1), n_hist):
            acc = acc + row(j)[:, cols]
        sums.append(acc)
        inv_cnt.append(1.0 / float(min(PAST_LEN + 1, w)))
    _pool_mix(sums, u, inv_cnt, pw_ref, ps_ref, o_ref, gb)
    nb_ref[:, 0:(n_hist - 1) * d_b] = hist_ref[:, d_b:]
    nb_ref[:, (n_hist - 1) * d_b:] = u


def _pool_sample(hist2d, z, pool_w, pool_scale, ns, d_b, u_row_block, u_col_block, n_hist):
    return pl.pallas_call(
        functools.partial(_pool_sample_kernel, n_hist=n_hist),
        grid=(1,),
        in_specs=[pl.BlockSpec(hist2d.shape, lambda i: (0, 0)),
                  pl.BlockSpec((ns, d_b), lambda i: (u_row_block, u_col_block)),
                  pl.BlockSpec(pool_w.shape, lambda i: (0, 0, 0)),
                  pl.BlockSpec((1, d_b), lambda i: (0, 0))],
        out_specs=[pl.BlockSpec((ns, d_b), lambda i: (0, 0)),
                   pl.BlockSpec(hist2d.shape, lambda i: (0, 0))],
        out_shape=[jax.ShapeDtypeStruct((ns, d_b), BF16),
                   jax.ShapeDtypeStruct(hist2d.shape, F32)],
        compiler_params=_cparams(("arbitrary",)),
        name="pool_sample",
    )(hist2d, z, pool_w, pool_scale.reshape(1, d_b))


def _spatial_kernel(u_ref, v_ref, wsp_ref, bt_ref, w00_ref, b0_ref, o_ref, *, n_prompt_tiles, n_heads):
    i = pl.program_id(0)
    rows, d_c = u_ref.shape
    gc = d_c // n_heads

    @pl.when(i < n_prompt_tiles)
    def _():
        ri = lax.broadcasted_iota(jnp.int32, (C_CHUNK, C_CHUNK), 0)
        ci = lax.broadcasted_iota(jnp.int32, (C_CHUNK, C_CHUNK), 1)
        for h in range(n_heads):
            wc = jnp.where(ri >= ci, wsp_ref[h], 0.0).astype(BF16)
            bias = bt_ref[:, h:h + 1]
            for c in range(rows // C_CHUNK):
                rs = slice(c * C_CHUNK, (c + 1) * C_CHUNK)
                cs = slice(h * gc, (h + 1) * gc)
                mixed = jnp.dot(wc, v_ref[rs, cs].astype(BF16), preferred_element_type=F32) + bias
                o_ref[rs, cs] = (u_ref[rs, cs] * mixed).astype(o_ref.dtype)

    @pl.when(i >= n_prompt_tiles)
    def _():
        o_ref[...] = (u_ref[...] * (v_ref[...] * w00_ref[...] + b0_ref[...])).astype(o_ref.dtype)


def _spatial(u, v, w_sp, b_sp, n_prompt_tiles):
    ntp, d_c = u.shape
    n_heads = w_sp.shape[0]
    gc = d_c // n_heads
    w00 = jnp.repeat(w_sp[:, 0, 0], gc).reshape(1, d_c)
    b0 = jnp.repeat(b_sp[:, 0], gc).reshape(1, d_c)
    tile = pl.BlockSpec((TM, d_c), lambda i: (i, 0))
    return pl.pallas_call(
        functools.partial(_spatial_kernel, n_prompt_tiles=n_prompt_tiles, n_heads=n_heads),
        grid=(ntp // TM,),
        in_specs=[tile, tile,
                  pl.BlockSpec(w_sp.shape, lambda i: (0, 0, 0)),
                  pl.BlockSpec((C_CHUNK, n_heads), lambda i: (0, 0)),
                  pl.BlockSpec((1, d_c), lambda i: (0, 0)),
                  pl.BlockSpec((1, d_c), lambda i: (0, 0))],
        out_specs=tile,
        out_shape=jax.ShapeDtypeStruct((ntp, d_c), BF16),
        compiler_params=_cparams(("arbitrary",)),
        name="spatial",
    )(u, v, w_sp, b_sp.T, w00, b0)


def _outproj_kernel(*refs, n_lhs, n_prompt_tiles, alpha):
    lhs = refs[:n_lhs]
    (x_ref, g1s, g1p, sc2s, sc2p, sh2s, sh2p, lng_ref, lnb_ref, wr_ref, w_hbm,
     x1_ref, h2_ref, lg_ref, h2p_ref, w_bf, stage, sem, out_a, out_b) = refs[n_lhs:]
    i = pl.program_id(0)
    tm, d = x_ref.shape

    @pl.when(i == 0)
    def _():
        _load_weight_bf16(w_hbm, w_bf, stage, sem, stage.shape[1])
        out_b[...] = jnp.zeros(out_b.shape, F32)

    is_s = i - 1 >= n_prompt_tiles

    def body(out_w, out_r):
        out = None
        k0 = 0
        for a_ref in lhs:
            kk = a_ref.shape[1]
            part = jnp.dot(a_ref[...], w_bf[k0:k0 + kk, :], preferred_element_type=F32)
            out = part if out is None else out + part
            k0 += kk
        out_w[...] = out
        for r0 in range(0, tm, ROW_CHUNK):
            rs = slice(r0, r0 + ROW_CHUNK)
            g1 = jnp.where(is_s, g1s[rs, :], g1p[...])
            x1 = _layer_norm(alpha * x_ref[rs, :] + g1 * out_r[rs, :], lng_ref[...], lnb_ref[...])
            x1_ref[rs, :] = x1
            sc2 = jnp.where(is_s, sc2s[rs, :], sc2p[...])
            sh2 = jnp.where(is_s, sh2s[rs, :], sh2p[...])
            h2_ref[rs, :] = x1 * (1.0 + sc2) + sh2
        lg_ref[...] = lax.dot_general(wr_ref[...], h2_ref[...], (((1,), (1,)), ((), ())),
                                      precision=lax.Precision.HIGHEST, preferred_element_type=F32)
        pitch = h2p_ref.shape[0] // tm
        rpx = d // (2 * LANES)
        for j in range(rpx):
            lo = lax.bitcast_convert_type(h2_ref[:, j * LANES:(j + 1) * LANES].astype(BF16).astype(F32), jnp.uint32)
            hi = lax.bitcast_convert_type(
                h2_ref[:, d // 2 + j * LANES:d // 2 + (j + 1) * LANES].astype(BF16).astype(F32), jnp.uint32)
            h2p_ref[pl.ds(j, tm, stride=pitch), :] = jnp.bitwise_or(hi, lax.shift_right_logical(lo, jnp.uint32(16)))
        for j in range(rpx, pitch):
            h2p_ref[pl.ds(j, tm, stride=pitch), :] = jnp.zeros((tm, LANES), jnp.uint32)

    @pl.when(i % 2 == 0)
    def _():
        body(out_a, out_b)

    @pl.when(i % 2 == 1)
    def _():
        body(out_b, out_a)


def _outproj(lhs, x_all, mod_s, mod_p, l, w_out, ln_g, ln_b, w_router_t, n_prompt_tiles, tiles_per_batch, nb, alpha):
    ntp, d = x_all.shape
    n_e = w_router_t.shape[0]
    ck = 256
    nt = ntp // TM
    xpitch = _pitch(d // (2 * LANES))
    lag = lambda i: jnp.maximum(i - 1, 0)
    tile = pl.BlockSpec((TM, d), lambda i: (lag(i), 0))
    specs = [pl.BlockSpec((TM, a.shape[1]), lambda i: (jnp.minimum(i, nt - 1), 0)) for a in lhs] + [tile]
    for j in (2, 4, 3):
        specs += _mod_specs(l, j, d, TM, n_prompt_tiles, tiles_per_batch, nb, lag=1)
    specs += [pl.BlockSpec((1, d), lambda i: (0, 0))] * 2
    specs += [pl.BlockSpec((n_e, d), lambda i: (0, 0)), pl.BlockSpec(memory_space=pl.ANY)]
    return pl.pallas_call(
        functools.partial(_outproj_kernel, n_lhs=len(lhs), n_prompt_tiles=n_prompt_tiles, alpha=alpha),
        grid=(nt + 1,),
        in_specs=specs,
        out_specs=[tile, tile, pl.BlockSpec((n_e, TM), lambda i: (0, lag(i))),
                   pl.BlockSpec((TM * xpitch, LANES), lambda i: (lag(i), 0))],
        out_shape=[jax.ShapeDtypeStruct((ntp, d), F32), jax.ShapeDtypeStruct((ntp, d), F32),
                   jax.ShapeDtypeStruct((n_e, ntp), F32),
                   jax.ShapeDtypeStruct((ntp * xpitch, LANES), jnp.uint32)],
        scratch_shapes=[pltpu.VMEM(w_out.shape, BF16), pltpu.VMEM((2, ck, d), F32), pltpu.SemaphoreType.DMA((2,)),
                        pltpu.VMEM((TM, d), F32), pltpu.VMEM((TM, d), F32)],
        compiler_params=_cparams(("arbitrary",)),
        name="outproj",
    )(*lhs, x_all, *([mod_s, mod_p] * 3), ln_g.reshape(1, d), ln_b.reshape(1, d), w_router_t, w_out)


def _first_argmax(vals, iota, axis, size):
    m = jnp.max(vals, axis=axis, keepdims=True)
    idx = jnp.min(jnp.where(vals == m, iota, size), axis=axis, keepdims=True)
    return m, idx


def _route_kernel(lg_ref, bias_ref, e_ref, w_ref):
    n_e, tm = lg_ref.shape
    gsz = n_e // N_GROUPS
    neg = -jnp.inf
    scores = jax.nn.sigmoid(lg_ref[...])
    sel = scores + bias_ref[...]
    sel3 = sel.reshape(N_GROUPS, gsz, tm)
    io3 = lax.broadcasted_iota(jnp.int32, sel3.shape, 1)
    m1, i1 = _first_argmax(sel3, io3, 1, gsz)
    m2 = jnp.max(jnp.where(io3 == i1, neg, sel3), axis=1, keepdims=True)
    gscore = (m1 + m2).reshape(N_GROUPS, tm)
    iog = lax.broadcasted_iota(jnp.int32, gscore.shape, 0)
    gsel = jnp.zeros(gscore.shape, F32)
    cur = gscore
    for _ in range(TOPK_GROUPS):
        _, gi = _first_argmax(cur, iog, 0, N_GROUPS)
        pick = iog == gi
        gsel = jnp.where(pick, 1.0, gsel)
        cur = jnp.where(pick, neg, cur)
    esel = jnp.broadcast_to(gsel.reshape(N_GROUPS, 1, tm), sel3.shape).reshape(n_e, tm)
    ioe = lax.broadcasted_iota(jnp.int32, sel.shape, 0)
    iok = lax.broadcasted_iota(jnp.int32, (TOP_K, tm), 0)
    cur = jnp.where(esel > 0.5, sel, neg)
    eidx = jnp.zeros((TOP_K, tm), jnp.int32)
    sc = jnp.zeros((TOP_K, tm), F32)
    for r in range(TOP_K):
        _, ei = _first_argmax(cur, ioe, 0, n_e)
        pick = ioe == ei
        val = jnp.sum(jnp.where(pick, scores, 0.0), axis=0, keepdims=True)
        eidx = jnp.where(iok == r, ei, eidx)
        sc = jnp.where(iok == r, val, sc)
        cur = jnp.where(pick, neg, cur)
    e_ref[...] = eidx
    w_ref[...] = sc / jnp.sum(sc, axis=0, keepdims=True) * ROUTED_SCALE


def _route(logits_t, router_bias):
    n_e, ntp = logits_t.shape
    return pl.pallas_call(
        _route_kernel,
        grid=(ntp // TM,),
        in_specs=[pl.BlockSpec((n_e, TM), lambda i: (0, i)), pl.BlockSpec((n_e, 1), lambda i: (0, 0))],
        out_specs=[pl.BlockSpec((TOP_K, TM), lambda i: (0, i))] * 2,
        out_shape=[jax.ShapeDtypeStruct((TOP_K, ntp), jnp.int32), jax.ShapeDtypeStruct((TOP_K, ntp), F32)],
        compiler_params=_cparams(("arbitrary",)),
        name="route",
    )(logits_t, router_bias.reshape(n_e, 1))


def _dispatch_tables(eidx_t, n_e):
    flat_e = eidx_t.T.reshape(-1)
    nk = flat_e.shape[0]
    n_steps = nk // BM + n_e + 1
    abits = (nk + BM - 1).bit_length()
    experts = jnp.arange(n_e, dtype=jnp.int32)
    counts = jnp.sum((flat_e[:, None] == experts[None, :]).astype(jnp.int32), axis=0)
    nblk = (counts + BM - 1) // BM
    bends = jnp.cumsum(nblk)
    total = bends[-1]
    j = jnp.arange(BM, dtype=jnp.int32)
    real = jnp.left_shift(flat_e, abits) | jnp.arange(nk, dtype=jnp.int32)
    pad = jnp.where(j[None, :] < (nblk * BM - counts)[:, None],
                    jnp.left_shift(experts, abits)[:, None] | (nk + j)[None, :], jnp.iinfo(jnp.int32).max)
    ids = jnp.sort(jnp.concatenate([real, pad.reshape(-1)])) & ((1 << abits) - 1)
    table = jnp.concatenate([nk + j, ids] + [nk + j] * (NSLOT - 1))
    g = jnp.arange(n_steps, dtype=jnp.int32)
    e_of = jnp.sum((bends[None, :] <= g[:, None]).astype(jnp.int32), axis=1)
    e_last = jnp.sum((bends <= total - 1).astype(jnp.int32))
    blk_e = jnp.minimum(jnp.where(g < total, e_of, e_last), n_e - 1)
    return blk_e.astype(jnp.int32), total.reshape(1).astype(jnp.int32), table


def _experts_kernel(blk_e, total_ref, table, h2_hbm, wg_ref, wu_ref, wd_ref, y_hbm,
                    xbuf, ybuf, xb, hb, wg_bf, wu_bf, wd_bf, gsem, ssem, *, ntp, nc):
    g = pl.program_id(0)
    total = total_ref[0]
    s = g % NSLOT
    s1 = (g + 1) % NSLOT
    s2 = (g + 2) % NSLOT
    kbits = TOP_K.bit_length() - 1
    d = xb.shape[1]
    rpt = d // LANES
    pitch = ybuf.shape[1] // BM

    rpx = rpt // 2
    xpitch = xbuf.shape[1] // BM

    def gather(j, slot, r):
        a = table[(j + 1) * BM + r]
        tok = jnp.minimum(lax.shift_right_logical(a, kbits), ntp - 1)
        return pltpu.make_async_copy(h2_hbm.at[pl.ds(tok * xpitch, rpx)], xbuf.at[slot, pl.ds(r * xpitch, rpx)],
                                     gsem.at[slot])

    def scatter(j, slot, r):
        a = table[(j + 1) * BM + r]
        return pltpu.make_async_copy(ybuf.at[slot, pl.ds(r * pitch, rpt)],
                                     y_hbm.at[pl.ds(pl.multiple_of(a * rpt, rpt), rpt)], ssem.at[slot])

    def wait_gather(slot):
        for r in range(BM):
            pltpu.make_async_copy(h2_hbm.at[pl.ds(0, rpx)], xbuf.at[slot, pl.ds(r * xpitch, rpx)], gsem.at[slot]).wait()

    def wait_scatter(slot):
        for r in range(BM):
            pltpu.make_async_copy(ybuf.at[slot, pl.ds(r * pitch, rpt)], y_hbm.at[pl.ds(0, rpt)], ssem.at[slot]).wait()

    @pl.when(g == 0)
    def _():
        ybuf[NSLOT - 1] = jnp.zeros(ybuf.shape[1:], F32)

        def body(r, c):
            gather(0, 0, r).start()
            gather(1, 1, r).start(priority=1)
            return c
        lax.fori_loop(0, BM, body, 0, unroll=8)

    @pl.when(g < total)
    def _():
        prev = blk_e[jnp.maximum(g - 1, 0)]

        @pl.when(jnp.logical_or(g == 0, blk_e[g] != prev))
        def _():
            wg_bf[...] = wg_ref[...].astype(BF16)
            wu_bf[...] = wu_ref[...].astype(BF16)
            wd_bf[...] = wd_ref[...].astype(BF16)

        wait_gather(s)

        @pl.when(g >= 2)
        def _():
            wait_scatter(s)

        d_e = wd_bf.shape[0]
        pieces = d_e // nc + d // nc
        rows = [list(range(p * BM // pieces, (p + 1) * BM // pieces)) for p in range(pieces)]

        def issue(p):
            for r in rows[p]:
                gather(g + 2, s2, r).start(priority=r % 2)
                scatter(g - 1, s2, r).start(priority=(r + 1) % 2)

        for c in range(rpx):
            u = xbuf[s, pl.ds(c, BM, stride=xpitch), :]
            lo = lax.bitcast_convert_type(lax.shift_left(u, jnp.uint32(16)), F32)
            hi = lax.bitcast_convert_type(jnp.bitwise_and(u, jnp.uint32(0xFFFF0000)), F32)
            xb[:, c * LANES:(c + 1) * LANES] = lo.astype(BF16)
            xb[:, d // 2 + c * LANES:d // 2 + (c + 1) * LANES] = hi.astype(BF16)
        p = 0
        for c in range(d_e // nc):
            cols = slice(c * nc, (c + 1) * nc)
            hg = jnp.dot(xb[...], wg_bf[:, cols], preferred_element_type=F32)
            hu = jnp.dot(xb[...], wu_bf[:, cols], preferred_element_type=F32)
            hb[:, cols] = (_silu(hg) * hu).astype(BF16)
            issue(p)
            p += 1
        for c in range(d // nc):
            y = jnp.dot(hb[...], wd_bf[:, c * nc:(c + 1) * nc], preferred_element_type=F32)
            for cc in range(nc // LANES):
                ybuf[s, pl.ds(c * (nc // LANES) + cc, BM, stride=pitch), :] = y[:, cc * LANES:(cc + 1) * LANES]
            issue(p)
            p += 1

    @pl.when(g == total)
    def _():
        wait_gather(s)
        wait_gather(s1)

        @pl.when(g >= 2)
        def _():
            wait_scatter(s)
        wait_scatter(s1)

        def body(r, c):
            scatter(g - 1, s2, r).start()
            return c
        lax.fori_loop(0, BM, body, 0, unroll=8)
        wait_scatter(s2)


def _experts(h2p, tables, w_gate, w_up, w_down, l):
    d, d_e = w_gate.shape[-2:]
    rpt = d // LANES
    pitch = _pitch(rpt)
    xpitch = _pitch(rpt // 2)
    ntp = h2p.shape[0] // xpitch
    n_steps = tables[0].shape[0]
    wmap = lambda g, be, *_: (l, be[g], 0, 0)
    return pl.pallas_call(
        functools.partial(_experts_kernel, ntp=ntp, nc=256),
        grid_spec=pltpu.PrefetchScalarGridSpec(
            num_scalar_prefetch=3, grid=(n_steps,),
            in_specs=[pl.BlockSpec(memory_space=pl.ANY),
                      pl.BlockSpec((None, None, d, d_e), wmap),
                      pl.BlockSpec((None, None, d, d_e), wmap),
                      pl.BlockSpec((None, None, d_e, d), wmap)],
            out_specs=pl.BlockSpec(memory_space=pl.ANY),
            scratch_shapes=[pltpu.VMEM((NSLOT, BM * xpitch, LANES), jnp.uint32),
                            pltpu.VMEM((NSLOT, BM * pitch, LANES), F32),
                            pltpu.VMEM((BM, d), BF16), pltpu.VMEM((BM, d_e), BF16),
                            pltpu.VMEM((d, d_e), BF16), pltpu.VMEM((d, d_e), BF16), pltpu.VMEM((d_e, d), BF16),
                            pltpu.SemaphoreType.DMA((NSLOT,)), pltpu.SemaphoreType.DMA((NSLOT,))]),
        out_shape=jax.ShapeDtypeStruct(((ntp * TOP_K + BM) * rpt, LANES), F32),
        compiler_params=_cparams(("arbitrary",)),
        name="experts",
    )(*tables, h2p, w_gate, w_up, w_down)


def _final_kernel(*refs, l, n_prompt_tiles, alpha, split):
    (y_hbm, w_ref, x1_ref, h2_ref, g2s, g2p, lng_ref, lnb_ref, wg_hbm, wu_hbm, wd_hbm) = refs[:11]
    if split:
        yp_ref, ysm_ref = refs[11:13]
        wg_bf, wu_bf, wd_bf, stage_a, stage_b, sem, ybuf, ysem, ffn_s, x2_s = refs[13:]
    else:
        x2_ref = refs[11]
        wg_bf, wu_bf, wd_bf, stage_a, stage_b, sem, ybuf, ysem, ffn_s = refs[12:]
    i = pl.program_id(0)
    tm, d = x1_ref.shape
    rpt = d // LANES
    rec = TOP_K * rpt
    pitch = ybuf.shape[1] // tm
    slot = i % 2

    def fetch(tile, sl, t):
        src = y_hbm.at[pl.ds(pl.multiple_of((tile * tm + t) * rec, rec), rec)]
        return pltpu.make_async_copy(src, ybuf.at[sl, pl.ds(t * pitch, rec)], ysem.at[sl])

    @pl.when(i == 0)
    def _():
        for t in range(tm):
            fetch(0, 0, t).start()
        _load_weight_bf16(wg_hbm.at[l], wg_bf, stage_a, sem, stage_a.shape[1])
        _load_weight_bf16(wu_hbm.at[l], wu_bf, stage_a, sem, stage_a.shape[1])
        _load_weight_bf16(wd_hbm.at[l], wd_bf, stage_b, sem, stage_b.shape[1])

    @pl.when(i + 1 < pl.num_programs(0))
    def _():
        for t in range(tm):
            fetch(i + 1, 1 - slot, t).start()

    hb = h2_ref[...].astype(BF16)
    sg = jnp.dot(hb, wg_bf[...], preferred_element_type=F32)
    su = jnp.dot(hb, wu_bf[...], preferred_element_type=F32)
    shared = jnp.dot((_silu(sg) * su).astype(BF16), wd_bf[...], preferred_element_type=F32)
    is_s = i >= n_prompt_tiles
    dst = x2_s if split else x2_ref

    for t in range(tm):
        fetch(0, slot, t).wait()
    w = w_ref[...]
    wk = [jnp.broadcast_to(w[:, k:k + 1], (tm, LANES)) for k in range(TOP_K)]
    for c in range(rpt):
        acc = shared[:, c * LANES:(c + 1) * LANES]
        for k in range(TOP_K):
            acc = acc + wk[k] * ybuf[slot, pl.ds(k * rpt + c, tm, stride=pitch), :]
        ffn_s[:, c * LANES:(c + 1) * LANES] = acc

    for r0 in range(0, tm, ROW_CHUNK):
        rs = slice(r0, r0 + ROW_CHUNK)
        g2 = jnp.where(is_s, g2s[rs, :], g2p[...])
        dst[rs, :] = _layer_norm(alpha * x1_ref[rs, :] + g2 * ffn_s[rs, :], lng_ref[...], lnb_ref[...])

    if split:
        @pl.when(i < n_prompt_tiles)
        def _():
            yp_ref[...] = x2_s[...]

        @pl.when(i == n_prompt_tiles)
        def _():
            ysm_ref[...] = x2_s[...]


def _final(yslots, wsel, x1, h2, mod_s, mod_p, l, ln_g, ln_b, ws_gate, ws_up, ws_down,
           n_prompt_rows, rows_per_batch, nb, alpha, split):
    ntp, d = x1.shape
    tm = TM_FINAL
    d_s = ws_gate.shape[-1]
    npt = n_prompt_rows // tm
    tpb = rows_per_batch // tm
    nt = ntp // tm
    tile = pl.BlockSpec((tm, d), lambda i: (i, 0))
    specs = [pl.BlockSpec(memory_space=pl.ANY), pl.BlockSpec((tm, TOP_K), lambda i: (i, 0)), tile, tile]
    specs += _mod_specs(l, 5, d, tm, npt, tpb, nb)
    specs += [pl.BlockSpec((1, d), lambda i: (0, 0))] * 2
    specs += [pl.BlockSpec(memory_space=pl.ANY)] * 3
    if split:
        out_specs = [pl.BlockSpec((tm, d), lambda i: (jnp.minimum(i, npt - 1), 0)),
                     pl.BlockSpec((tm, d), lambda i: (0, 0))]
        out_shape = [jax.ShapeDtypeStruct((n_prompt_rows, d), F32), jax.ShapeDtypeStruct((tm, d), F32)]
    else:
        out_specs = tile
        out_shape = jax.ShapeDtypeStruct((ntp, d), F32)
    ck = 256
    scratch = [pltpu.VMEM((d, d_s), BF16), pltpu.VMEM((d, d_s), BF16), pltpu.VMEM((d_s, d), BF16),
               pltpu.VMEM((2, ck, d_s), F32), pltpu.VMEM((2, ck, d), F32), pltpu.SemaphoreType.DMA((2,)),
               pltpu.VMEM((2, tm * _pitch(TOP_K * d // LANES), LANES), F32), pltpu.SemaphoreType.DMA((2,)),
               pltpu.VMEM((tm, d), F32)]
    if split:
        scratch.append(pltpu.VMEM((tm, d), F32))
    return pl.pallas_call(
        functools.partial(_final_kernel, l=l, n_prompt_tiles=npt, alpha=alpha, split=split),
        grid=(nt,),
        in_specs=specs, out_specs=out_specs, out_shape=out_shape,
        scratch_shapes=scratch,
        compiler_params=_cparams(("arbitrary",)),
        name="moe_final",
    )(yslots, wsel, x1, h2, mod_s, mod_p, ln_g.reshape(1, d), ln_b.reshape(1, d),
      ws_gate, ws_up, ws_down)


def kernel(x_prompt, x_sample, state_hgrn, state_pool, c_prompt, c_sample, mix0_w_in, mix0_w_out, hgrn_lb_table, hgrn_rms_w, pool_w, pool_scale, mix1_w_in, gmlp_ln_g, gmlp_ln_b, gmlp_w_sp, gmlp_b_sp, mix1_w_out, ada_w, ada_b, ln_mix_g, ln_mix_b, ln_ffn_g, ln_ffn_b, moe_w_router, moe_router_bias, moe_w_gate, moe_w_up, moe_w_down, shared_w_gate, shared_w_up, shared_w_down):
    nb, t, d = x_prompt.shape
    ns = x_sample.shape[0]
    depth = ada_w.shape[0]
    n_heads, dk = state_hgrn.shape[2], state_hgrn.shape[3]
    d_a = n_heads * dk
    d_b = state_pool.shape[3]
    n_hist = state_pool.shape[2]
    hist_pad = 16
    n_e = moe_w_router.shape[2]
    alpha = (2.0 * depth) ** 0.25
    n_p = nb * t
    npt = n_p // TM
    tpb = t // TM
    assert t % 512 == 0 and ns <= TM and ns % 8 == 0 and n_hist < hist_pad and d_a % d_b == 0

    x_all = jnp.concatenate([x_prompt.reshape(n_p, d), x_sample.reshape(ns, d), jnp.zeros((TM - ns, d), F32)], axis=0)
    c_all = jnp.concatenate([c_sample, jnp.zeros((TM - ns, d), F32), c_prompt, jnp.zeros((8 - nb % 8, d), F32)], axis=0)
    mod_s, mod_p = _adaln(c_all, ada_w, ada_b, TM)
    mod_p = mod_p.reshape(depth, mod_p.shape[1], 1, N_MOD * d)

    hgrn_p, pool_p, hgrn_s, pool_s, chunk_v = [], [], [], [], []
    x = x_all
    for l in range(depth):
        if l % 2 == 0:
            e = l // 2
            z = _inproj(x, mod_s, mod_p, l, mix0_w_in[e], npt, tpb, nb)
            o_a_p, s_p = _gla_prompt(z, hgrn_lb_table, hgrn_rms_w[e], l, nb, t, n_heads, dk)
            o_b_p, tail = _pool_prompt(z, pool_w[e], pool_scale[e], nb, t, d_b, 4 * d_a // d_b, hist_pad)
            hgrn_p.append(s_p)
            pool_p.append(tail[:, hist_pad - n_hist:])
            zs = z[n_p:n_p + ns]
            qt = zs[:, 0:d_a].reshape(ns, n_heads, dk).transpose(1, 2, 0)
            ft = zs[:, d_a:2 * d_a].reshape(ns, n_heads, dk).transpose(1, 2, 0)
            lbt_col = hgrn_lb_table.reshape(-1, n_heads, dk, 1)
            o_a_s, s_s = _gla_sample(qt, ft, lbt_col, zs[:, 2 * d_a:3 * d_a], zs[:, 3 * d_a:4 * d_a],
                                     hgrn_rms_w[e], state_hgrn[e], l)
            o_b_s, buf_s = _pool_sample(state_pool[e].reshape(ns, n_hist * d_b), z, pool_w[e], pool_scale[e],
                                        ns, d_b, n_p // ns, 4 * d_a // d_b, n_hist)
            hgrn_s.append(s_s)
            pool_s.append(buf_s.reshape(ns, n_hist, d_b))
            pad = jnp.zeros((TM - ns, d_a), BF16)
            o_a = jnp.concatenate([o_a_p, o_a_s.astype(BF16), pad], axis=0)
            o_b = jnp.concatenate([o_b_p, o_b_s, jnp.zeros((TM - ns, d_b), BF16)], axis=0)
            lhs, w_out = [o_a, o_b], mix0_w_out[e]
        else:
            o = l // 2
            u, v = _inproj(x, mod_s, mod_p, l, mix1_w_in[o], npt, tpb, nb, ln=(gmlp_ln_g[o], gmlp_ln_b[o]))
            chunk_v.append(v[n_p:n_p + ns].reshape(ns, 1, -1))
            lhs, w_out = [_spatial(u, v, gmlp_w_sp[o], gmlp_b_sp[o], npt)], mix1_w_out[o]
        x1, h2, logits_t, h2p = _outproj(lhs, x, mod_s, mod_p, l, w_out, ln_mix_g[l], ln_mix_b[l],
                                    moe_w_router[l].T, npt, tpb, nb, alpha)
        eidx_t, wsel_t = _route(logits_t, moe_router_bias[l])
        yslots = _experts(h2p, _dispatch_tables(eidx_t, n_e), moe_w_gate, moe_w_up, moe_w_down, l)
        last = l == depth - 1
        res = _final(yslots, wsel_t.T, x1, h2, mod_s, mod_p, l, ln_ffn_g[l], ln_ffn_b[l],
                     shared_w_gate, shared_w_up, shared_w_down, n_p, t, nb, alpha, split=last)
        if last:
            y_p, y_s = res
        else:
            x = res
    return (y_p.reshape(nb, t, d), y_s[:ns].reshape(ns, 1, d),
            jnp.stack(hgrn_p), jnp.stack(pool_p), jnp.stack(hgrn_s), jnp.stack(pool_s), jnp.stack(chunk_v))
```

```python
import functools

import jax
import jax.numpy as jnp
from jax import lax
from jax.experimental import pallas as pl
from jax.experimental.pallas import tpu as pltpu

F32 = jnp.float32
BF16 = jnp.bfloat16

PAST_LEN = 16384
GLA_CHUNK = 32
POOL_WINDOWS = (2, 4, 8, 16)
C_CHUNK = 128
TOP_K = 8
N_GROUPS = 8
TOPK_GROUPS = 4
ROUTED_SCALE = 2.5
LN_EPS = 1e-5
RMS_EPS = 1e-6
N_MOD = 6

TM = 256
TM_FINAL = 128
BM = 256
NSLOT = 3
ROW_CHUNK = 32
LANES = 128


def _pitch(rows):
    assert rows % 8 == 0
    return rows + 4
VMEM_LIMIT = 56 * 1024 * 1024


def _cparams(sem):
    return pltpu.CompilerParams(dimension_semantics=sem, vmem_limit_bytes=VMEM_LIMIT)


def _silu(x):
    return x * jax.nn.sigmoid(x)


def _layer_norm(y, g, b):
    mu = jnp.mean(y, axis=-1, keepdims=True)
    d = y - mu
    var = jnp.mean(d * d, axis=-1, keepdims=True)
    return d * lax.rsqrt(var + LN_EPS) * g + b


def _load_weight_bf16(w_hbm, w_bf, stage, sem, ck):
    n = w_hbm.shape[0] // ck

    def cp(c, slot):
        return pltpu.make_async_copy(w_hbm.at[pl.ds(pl.multiple_of(c * ck, ck), ck)], stage.at[slot], sem.at[slot])

    cp(0, 0).start()

    def body(c, carry):
        slot = c % 2

        @pl.when(c + 1 < n)
        def _():
            cp(c + 1, 1 - slot).start()

        cp(c, slot).wait()
        w_bf[pl.ds(pl.multiple_of(c * ck, ck), ck), :] = stage[slot].astype(BF16)
        return carry

    lax.fori_loop(0, n, body, 0)


def _mod_specs(l, j, d, tm, n_prompt_tiles, tiles_per_batch, nb, lag=0):
    tile = lambda i: jnp.maximum(i - lag, 0)
    s = pl.BlockSpec((None, tm, d), lambda i: (l, jnp.maximum(tile(i) - n_prompt_tiles, 0), j))
    p = pl.BlockSpec((None, None, 1, d), lambda i: (l, jnp.minimum(tile(i) // tiles_per_batch, nb - 1), 0, j))
    return [s, p]


def _adaln_kernel(c_ref, w_ref, b_ref, os_ref, op_ref, *, ns_rows):
    c = c_ref[...]
    s = _silu(c).astype(BF16)
    r = jnp.dot(s, w_ref[...].astype(BF16), preferred_element_type=F32) + b_ref[...]
    os_ref[...] = r[:ns_rows]
    op_ref[...] = r[ns_rows:]


def _adaln(c_all, ada_w, ada_b, ns_rows):
    depth, d, n6 = ada_w.shape
    cm = c_all.shape[0]
    tn = 1024
    return pl.pallas_call(
        functools.partial(_adaln_kernel, ns_rows=ns_rows),
        grid=(depth, n6 // tn),
        in_specs=[pl.BlockSpec((cm, d), lambda l, j: (0, 0)),
                  pl.BlockSpec((None, d, tn), lambda l, j: (l, 0, j)),
                  pl.BlockSpec((None, 1, tn), lambda l, j: (l, 0, j))],
        out_specs=[pl.BlockSpec((None, ns_rows, tn), lambda l, j: (l, 0, j)),
                   pl.BlockSpec((None, cm - ns_rows, tn), lambda l, j: (l, 0, j))],
        out_shape=[jax.ShapeDtypeStruct((depth, ns_rows, n6), F32),
                   jax.ShapeDtypeStruct((depth, cm - ns_rows, n6), F32)],
        compiler_params=_cparams(("arbitrary", "arbitrary")),
        name="adaln",
    )(c_all, ada_w, ada_b.reshape(depth, 1, n6))


def _inproj_kernel(x_ref, scs_ref, scp_ref, shs_ref, shp_ref, w_hbm, *rest, n_prompt_tiles, nc, odd):
    if odd:
        lng_ref, lnb_ref, u_ref, v_ref, w_bf, hb, stage, sem = rest
    else:
        z_ref, w_bf, hb, stage, sem = rest
    i = pl.program_id(0)

    @pl.when(i == 0)
    def _():
        _load_weight_bf16(w_hbm, w_bf, stage, sem, stage.shape[1])

    is_s = i >= n_prompt_tiles
    for r0 in range(0, hb.shape[0], ROW_CHUNK):
        rs = slice(r0, r0 + ROW_CHUNK)
        sc = jnp.where(is_s, scs_ref[rs, :], scp_ref[...])
        sh = jnp.where(is_s, shs_ref[rs, :], shp_ref[...])
        hb[rs, :] = (x_ref[rs, :] * (1.0 + sc) + sh).astype(BF16)
    n = w_bf.shape[1]
    for c in range(n // nc):
        z = jnp.dot(hb[...], w_bf[:, c * nc:(c + 1) * nc], preferred_element_type=F32)
        if not odd:
            z_ref[:, c * nc:(c + 1) * nc] = z
        else:
            z = jax.nn.gelu(z)
            half = n // 2
            if c * nc < half:
                u_ref[:, c * nc:(c + 1) * nc] = z
            else:
                v_ref[:, c * nc - half:(c + 1) * nc - half] = z
    if odd:
        v_ref[...] = _layer_norm(v_ref[...], lng_ref[...], lnb_ref[...])


def _inproj(x_all, mod_s, mod_p, l, w, n_prompt_tiles, tiles_per_batch, nb, ln=None):
    ntp, d = x_all.shape
    n = w.shape[1]
    odd = ln is not None
    ck, nc = 128, 1024
    specs = [pl.BlockSpec((TM, d), lambda i: (i, 0))]
    specs += _mod_specs(l, 1, d, TM, n_prompt_tiles, tiles_per_batch, nb)
    specs += _mod_specs(l, 0, d, TM, n_prompt_tiles, tiles_per_batch, nb)
    specs += [pl.BlockSpec(memory_space=pl.ANY)]
    args = [x_all, mod_s, mod_p, mod_s, mod_p, w]
    if odd:
        specs += [pl.BlockSpec((1, n // 2), lambda i: (0, 0))] * 2
        args += [ln[0].reshape(1, -1), ln[1].reshape(1, -1)]
        out_specs = [pl.BlockSpec((TM, n // 2), lambda i: (i, 0))] * 2
        out_shape = [jax.ShapeDtypeStruct((ntp, n // 2), F32)] * 2
    else:
        out_specs = pl.BlockSpec((TM, n), lambda i: (i, 0))
        out_shape = jax.ShapeDtypeStruct((ntp, n), F32)
    return pl.pallas_call(
        functools.partial(_inproj_kernel, n_prompt_tiles=n_prompt_tiles, nc=nc, odd=odd),
        grid=(ntp // TM,),
        in_specs=specs, out_specs=out_specs, out_shape=out_shape,
        scratch_shapes=[pltpu.VMEM((d, n), BF16), pltpu.VMEM((TM, d), BF16),
                        pltpu.VMEM((2, ck, n), F32), pltpu.SemaphoreType.DMA((2,))],
        compiler_params=_cparams(("arbitrary",)),
        name="inproj_odd" if odd else "inproj_even",
    )(*args)


def _forget_lower_bound(rows, l):
    m = functools.reduce(jnp.maximum, rows)
    e = [jnp.exp(r - m) for r in rows]
    return sum(e[:l + 1]) / sum(e)


def _gla_prompt_kernel(q_ref, f_ref, v_ref, g_ref, lbt_ref, rms_ref, o_ref, s_ref,
                       st, qe_s, ke_s, kd_s, es_s, o_s, *, l, n_heads, dk):
    tb = pl.program_id(1)
    rows = q_ref.shape[0]
    nchunk = rows // GLA_CHUNK

    @pl.when(tb == 0)
    def _():
        st[...] = jnp.zeros_like(st)

    lb = _forget_lower_bound([lbt_ref[r:r + 1, :] for r in range(lbt_ref.shape[0])], l)
    fp = f_ref[...]
    logf = jnp.log(lb + (1.0 - lb) * jax.nn.sigmoid(fp))
    k = (1.0 - lb) * jax.nn.sigmoid(-fp)
    rid = lax.broadcasted_iota(jnp.int32, logf.shape, 0) % GLA_CHUNK
    bcum = logf
    s = 1
    while s < GLA_CHUNK:
        bcum = bcum + jnp.where(rid >= s, pltpu.roll(bcum, s, axis=0), 0.0)
        s *= 2
    b3 = bcum.reshape(nchunk, GLA_CHUNK, bcum.shape[1])
    blast = b3[:, GLA_CHUNK - 1:GLA_CHUNK, :]
    es_s[...] = jnp.exp(blast).reshape(nchunk, bcum.shape[1])
    blast_full = jnp.broadcast_to(blast, b3.shape).reshape(bcum.shape)
    qe_s[...] = q_ref[...] * jnp.exp(bcum)
    ke_s[...] = k * jnp.exp(-bcum)
    kd_s[...] = k * jnp.exp(blast_full - bcum)

    ri = lax.broadcasted_iota(jnp.int32, (GLA_CHUNK, GLA_CHUNK), 0)
    ci = lax.broadcasted_iota(jnp.int32, (GLA_CHUNK, GLA_CHUNK), 1)
    causal = ri >= ci

    def chunk_step(c, carry):
        r0 = pl.multiple_of(c * GLA_CHUNK, GLA_CHUNK)
        es_row = es_s[pl.ds(c, 1), :]
        for h in range(n_heads):
            cols = slice(h * dk, (h + 1) * dk)
            qe = qe_s[pl.ds(r0, GLA_CHUNK), cols]
            ke = ke_s[pl.ds(r0, GLA_CHUNK), cols]
            kd = kd_s[pl.ds(r0, GLA_CHUNK), cols]
            v = v_ref[pl.ds(r0, GLA_CHUNK), cols]
            att = lax.dot_general(qe, ke, (((1,), (1,)), ((), ())), preferred_element_type=F32)
            att = jnp.where(causal, att, 0.0)
            sth = st[h]
            o = jnp.dot(att, v, preferred_element_type=F32)
            o = o + lax.dot_general(qe, sth, (((1,), (1,)), ((), ())), preferred_element_type=F32)
            upd = lax.dot_general(v, kd, (((0,), (0,)), ((), ())), preferred_element_type=F32)
            st[h] = es_row[:, cols] * sth + upd
            o_s[pl.ds(r0, GLA_CHUNK), cols] = o
        return carry

    lax.fori_loop(0, nchunk, chunk_step, 0, unroll=4)

    for h in range(n_heads):
        cols = slice(h * dk, (h + 1) * dk)
        o = o_s[:, cols]
        o = o * lax.rsqrt(jnp.mean(o * o, axis=-1, keepdims=True) + RMS_EPS)
        o_ref[:, cols] = (o * rms_ref[:, cols] * _silu(g_ref[:, cols])).astype(o_ref.dtype)

    @pl.when(tb == pl.num_programs(1) - 1)
    def _():
        for h in range(n_heads):
            s_ref[0, h] = st[h].T


def _gla_prompt(z, lb_table, rms_w, l, nb, t, n_heads, dk):
    d_a = n_heads * dk
    tbk = 512
    nt = t // tbk
    row = lambda b, tb: b * nt + tb
    zspec = lambda sec: pl.BlockSpec((tbk, d_a), lambda b, tb: (row(b, tb), sec))
    return pl.pallas_call(
        functools.partial(_gla_prompt_kernel, l=l, n_heads=n_heads, dk=dk),
        grid=(nb, nt),
        in_specs=[zspec(0), zspec(1), zspec(2), zspec(3),
                  pl.BlockSpec(lb_table.shape, lambda b, tb: (0, 0)),
                  pl.BlockSpec((1, d_a), lambda b, tb: (0, 0))],
        out_specs=[pl.BlockSpec((tbk, d_a), lambda b, tb: (row(b, tb), 0)),
                   pl.BlockSpec((1, n_heads, dk, dk), lambda b, tb: (b, 0, 0, 0))],
        out_shape=[jax.ShapeDtypeStruct((nb * t, d_a), BF16),
                   jax.ShapeDtypeStruct((nb, n_heads, dk, dk), F32)],
        scratch_shapes=[pltpu.VMEM((n_heads, dk, dk), F32),
                        pltpu.VMEM((tbk, d_a), F32), pltpu.VMEM((tbk, d_a), F32), pltpu.VMEM((tbk, d_a), F32),
                        pltpu.VMEM((tbk // GLA_CHUNK, d_a), F32), pltpu.VMEM((tbk, d_a), F32)],
        compiler_params=_cparams(("arbitrary", "arbitrary")),
        name="gla_prompt",
    )(z, z, z, z, lb_table, rms_w.reshape(1, d_a))


def _gla_sample_kernel(qt_ref, ft_ref, lbt_ref, v_ref, g_ref, rms_ref, s0_ref, o_ref, s_ref, *, l):
    ns = s0_ref.shape[0]
    lbc = _forget_lower_bound([lbt_ref[r] for r in range(lbt_ref.shape[0])], l)
    fp = ft_ref[...]
    fcol = lbc + (1.0 - lbc) * jax.nn.sigmoid(fp)
    kcol = (1.0 - lbc) * jax.nn.sigmoid(-fp)
    qcol = qt_ref[...]
    for b in range(ns):
        v = v_ref[b:b + 1, :]
        s_new = fcol[:, b:b + 1] * s0_ref[b, 0] + kcol[:, b:b + 1] * v
        s_ref[b, 0] = s_new
        o_ref[b:b + 1, :] = jnp.sum(qcol[:, b:b + 1] * s_new, axis=0, keepdims=True)
    o = o_ref[...]
    o = o * lax.rsqrt(jnp.mean(o * o, axis=-1, keepdims=True) + RMS_EPS)
    o_ref[...] = o * rms_ref[...] * _silu(g_ref[...])


def _gla_sample(qt, ft, lbt_col, v_s, g_s, rms_w, s0, l):
    ns, n_heads, dk, dv = s0.shape
    return pl.pallas_call(
        functools.partial(_gla_sample_kernel, l=l),
        grid=(n_heads,),
        in_specs=[pl.BlockSpec((None, dk, ns), lambda h: (h, 0, 0)),
                  pl.BlockSpec((None, dk, ns), lambda h: (h, 0, 0)),
                  pl.BlockSpec((lbt_col.shape[0], None, dk, 1), lambda h: (0, h, 0, 0)),
                  pl.BlockSpec((ns, dv), lambda h: (0, h)),
                  pl.BlockSpec((ns, dv), lambda h: (0, h)),
                  pl.BlockSpec((1, dv), lambda h: (0, h)),
                  pl.BlockSpec((ns, 1, dk, dv), lambda h: (0, h, 0, 0))],
        out_specs=[pl.BlockSpec((ns, dv), lambda h: (0, h)),
                   pl.BlockSpec((ns, 1, dk, dv), lambda h: (0, h, 0, 0))],
        out_shape=[jax.ShapeDtypeStruct((ns, n_heads * dv), F32),
                   jax.ShapeDtypeStruct(s0.shape, F32)],
        compiler_params=_cparams(("arbitrary",)),
        name="gla_sample",
    )(qt, ft, lbt_col, v_s, g_s, rms_w.reshape(1, -1), s0)


def _pool_mix(win_sums, u, inv_cnt, pw_ref, ps_ref, o_ref, gb):
    for gi in range(len(POOL_WINDOWS)):
        cols = slice(gi * gb, (gi + 1) * gb)
        pooled = win_sums[gi] * inv_cnt[gi] - u[:, cols]
        mixed = jnp.dot(pooled.astype(BF16), pw_ref[gi].astype(BF16), preferred_element_type=F32)
        o_ref[:, cols] = (mixed * ps_ref[:, cols]).astype(o_ref.dtype)


def _pool_prompt_kernel(u_ref, pw_ref, ps_ref, o_ref, tail_ref, buf, *, hist_pad):
    tb = pl.program_id(1)
    rows, d_b = u_ref.shape
    gb = d_b // len(POOL_WINDOWS)

    @pl.when(tb == 0)
    def _():
        buf[0:hist_pad, :] = jnp.zeros((hist_pad, d_b), F32)

    u = u_ref[...]
    buf[hist_pad:, :] = u
    acc = buf[...]
    sums = []
    span = 1
    for gi, w in enumerate(POOL_WINDOWS):
        while span < w:
            acc = acc + pltpu.roll(acc, span, axis=0)
            span *= 2
        sums.append(acc[hist_pad:, gi * gb:(gi + 1) * gb])
    pos = tb * rows + lax.broadcasted_iota(jnp.int32, (rows, gb), 0)
    inv_cnt = [1.0 / jnp.minimum(pos + 1, w).astype(F32) for w in POOL_WINDOWS]
    _pool_mix(sums, u, inv_cnt, pw_ref, ps_ref, o_ref, gb)
    buf[0:hist_pad, :] = u[rows - hist_pad:, :]
    tail_ref[0] = u[rows - hist_pad:, :]


def _pool_prompt(z, pool_w, pool_scale, nb, t, d_b, u_col_block, hist_pad):
    tbk = 512
    nt = t // tbk
    return pl.pallas_call(
        functools.partial(_pool_prompt_kernel, hist_pad=hist_pad),
        grid=(nb, nt),
        in_specs=[pl.BlockSpec((tbk, d_b), lambda b, tb: (b * nt + tb, u_col_block)),
                  pl.BlockSpec(pool_w.shape, lambda b, tb: (0, 0, 0)),
                  pl.BlockSpec((1, d_b), lambda b, tb: (0, 0))],
        out_specs=[pl.BlockSpec((tbk, d_b), lambda b, tb: (b * nt + tb, 0)),
                   pl.BlockSpec((1, hist_pad, d_b), lambda b, tb: (b, 0, 0))],
        out_shape=[jax.ShapeDtypeStruct((nb * t, d_b), BF16),
                   jax.ShapeDtypeStruct((nb, hist_pad, d_b), F32)],
        scratch_shapes=[pltpu.VMEM((hist_pad + tbk, d_b), F32)],
        compiler_params=_cparams(("arbitrary", "arbitrary")),
        name="pool_prompt",
    )(z, pool_w, pool_scale.reshape(1, d_b))


def _pool_sample_kernel(hist_ref, u_ref, pw_ref, ps_ref, o_ref, nb_ref, *, n_hist):
    ns, d_b = u_ref.shape
    gb = d_b // len(POOL_WINDOWS)
    u = u_ref[...]
    row = lambda j: hist_ref[:, j * d_b:(j + 1) * d_b]
    sums, inv_cnt = [], []
    for gi, w in enumerate(POOL_WINDOWS):
        cols = slice(gi * gb, (gi + 1) * gb)
        acc = u[:, cols]
        for j in range(n_hist - (w - 1), n_hist):
            acc = acc + row(j)[:, cols]
        sums.append(acc)
        inv_cnt.append(1.0 / float(min(PAST_LEN + 1, w)))
    _pool_mix(sums, u, inv_cnt, pw_ref, ps_ref, o_ref, gb)
    nb_ref[:, 0:(n_hist - 1) * d_b] = hist_ref[:, d_b:]
    nb_ref[:, (n_hist - 1) * d_b:] = u


def _pool_sample(hist2d, z, pool_w, pool_scale, ns, d_b, u_row_block, u_col_block, n_hist):
    return pl.pallas_call(
        functools.partial(_pool_sample_kernel, n_hist=n_hist),
        grid=(1,),
        in_specs=[pl.BlockSpec(hist2d.shape, lambda i: (0, 0)),
                  pl.BlockSpec((ns, d_b), lambda i: (u_row_block, u_col_block)),
                  pl.BlockSpec(pool_w.shape, lambda i: (0, 0, 0)),
                  pl.BlockSpec((1, d_b), lambda i: (0, 0))],
        out_specs=[pl.BlockSpec((ns, d_b), lambda i: (0, 0)),
                   pl.BlockSpec(hist2d.shape, lambda i: (0, 0))],
        out_shape=[jax.ShapeDtypeStruct((ns, d_b), BF16),
                   jax.ShapeDtypeStruct(hist2d.shape, F32)],
        compiler_params=_cparams(("arbitrary",)),
        name="pool_sample",
    )(hist2d, z, pool_w, pool_scale.reshape(1, d_b))


def _spatial_kernel(u_ref, v_ref, wsp_ref, bt_ref, w00_ref, b0_ref, o_ref, *, n_prompt_tiles, n_heads):
    i = pl.program_id(0)
    rows, d_c = u_ref.shape
    gc = d_c // n_heads

    @pl.when(i < n_prompt_tiles)
    def _():
        ri = lax.broadcasted_iota(jnp.int32, (C_CHUNK, C_CHUNK), 0)
        ci = lax.broadcasted_iota(jnp.int32, (C_CHUNK, C_CHUNK), 1)
        for h in range(n_heads):
            wc = jnp.where(ri >= ci, wsp_ref[h], 0.0).astype(BF16)
            bias = bt_ref[:, h:h + 1]
            for c in range(rows // C_CHUNK):
                rs = slice(c * C_CHUNK, (c + 1) * C_CHUNK)
                cs = slice(h * gc, (h + 1) * gc)
                mixed = jnp.dot(wc, v_ref[rs, cs].astype(BF16), preferred_element_type=F32) + bias
                o_ref[rs, cs] = (u_ref[rs, cs] * mixed).astype(o_ref.dtype)

    @pl.when(i >= n_prompt_tiles)
    def _():
        o_ref[...] = (u_ref[...] * (v_ref[...] * w00_ref[...] + b0_ref[...])).astype(o_ref.dtype)


def _spatial(u, v, w_sp, b_sp, n_prompt_tiles):
    ntp, d_c = u.shape
    n_heads = w_sp.shape[0]
    gc = d_c // n_heads
    w00 = jnp.repeat(w_sp[:, 0, 0], gc).reshape(1, d_c)
    b0 = jnp.repeat(b_sp[:, 0], gc).reshape(1, d_c)
    tile = pl.BlockSpec((TM, d_c), lambda i: (i, 0))
    return pl.pallas_call(
        functools.partial(_spatial_kernel, n_prompt_tiles=n_prompt_tiles, n_heads=n_heads),
        grid=(ntp // TM,),
        in_specs=[tile, tile,
                  pl.BlockSpec(w_sp.shape, lambda i: (0, 0, 0)),
                  pl.BlockSpec((C_CHUNK, n_heads), lambda i: (0, 0)),
                  pl.BlockSpec((1, d_c), lambda i: (0, 0)),
                  pl.BlockSpec((1, d_c), lambda i: (0, 0))],
        out_specs=tile,
        out_shape=jax.ShapeDtypeStruct((ntp, d_c), BF16),
        compiler_params=_cparams(("arbitrary",)),
        name="spatial",
    )(u, v, w_sp, b_sp.T, w00, b0)


def _outproj_kernel(*refs, n_lhs, n_prompt_tiles, alpha):
    lhs = refs[:n_lhs]
    (x_ref, g1s, g1p, sc2s, sc2p, sh2s, sh2p, lng_ref, lnb_ref, wr_ref, w_hbm,
     x1_ref, h2_ref, lg_ref, h2p_ref, w_bf, stage, sem, out_a, out_b) = refs[n_lhs:]
    i = pl.program_id(0)
    tm, d = x_ref.shape

    @pl.when(i == 0)
    def _():
        _load_weight_bf16(w_hbm, w_bf, stage, sem, stage.shape[1])
        out_b[...] = jnp.zeros(out_b.shape, F32)

    is_s = i - 1 >= n_prompt_tiles

    def body(out_w, out_r):
        out = None
        k0 = 0
        for a_ref in lhs:
            kk = a_ref.shape[1]
            part = jnp.dot(a_ref[...], w_bf[k0:k0 + kk, :], preferred_element_type=F32)
            out = part if out is None else out + part
            k0 += kk
        out_w[...] = out
        for r0 in range(0, tm, ROW_CHUNK):
            rs = slice(r0, r0 + ROW_CHUNK)
            g1 = jnp.where(is_s, g1s[rs, :], g1p[...])
            x1 = _layer_norm(alpha * x_ref[rs, :] + g1 * out_r[rs, :], lng_ref[...], lnb_ref[...])
            x1_ref[rs, :] = x1
            sc2 = jnp.where(is_s, sc2s[rs, :], sc2p[...])
            sh2 = jnp.where(is_s, sh2s[rs, :], sh2p[...])
            h2_ref[rs, :] = x1 * (1.0 + sc2) + sh2
        lg_ref[...] = lax.dot_general(wr_ref[...], h2_ref[...], (((1,), (1,)), ((), ())),
                                      precision=lax.Precision.HIGHEST, preferred_element_type=F32)
        pitch = h2p_ref.shape[0] // tm
        rpt = d // LANES
        for j in range(rpt):
            h2p_ref[pl.ds(j, tm, stride=pitch), :] = h2_ref[:, j * LANES:(j + 1) * LANES]
        for j in range(rpt, pitch):
            h2p_ref[pl.ds(j, tm, stride=pitch), :] = jnp.zeros((tm, LANES), F32)

    @pl.when(i % 2 == 0)
    def _():
        body(out_a, out_b)

    @pl.when(i % 2 == 1)
    def _():
        body(out_b, out_a)


def _outproj(lhs, x_all, mod_s, mod_p, l, w_out, ln_g, ln_b, w_router_t, n_prompt_tiles, tiles_per_batch, nb, alpha):
    ntp, d = x_all.shape
    n_e = w_router_t.shape[0]
    ck = 256
    nt = ntp // TM
    xpitch = _pitch(d // LANES)
    lag = lambda i: jnp.maximum(i - 1, 0)
    tile = pl.BlockSpec((TM, d), lambda i: (lag(i), 0))
    specs = [pl.BlockSpec((TM, a.shape[1]), lambda i: (jnp.minimum(i, nt - 1), 0)) for a in lhs] + [tile]
    for j in (2, 4, 3):
        specs += _mod_specs(l, j, d, TM, n_prompt_tiles, tiles_per_batch, nb, lag=1)
    specs += [pl.BlockSpec((1, d), lambda i: (0, 0))] * 2
    specs += [pl.BlockSpec((n_e, d), lambda i: (0, 0)), pl.BlockSpec(memory_space=pl.ANY)]
    return pl.pallas_call(
        functools.partial(_outproj_kernel, n_lhs=len(lhs), n_prompt_tiles=n_prompt_tiles, alpha=alpha),
        grid=(nt + 1,),
        in_specs=specs,
        out_specs=[tile, tile, pl.BlockSpec((n_e, TM), lambda i: (0, lag(i))),
                   pl.BlockSpec((TM * xpitch, LANES), lambda i: (lag(i), 0))],
        out_shape=[jax.ShapeDtypeStruct((ntp, d), F32), jax.ShapeDtypeStruct((ntp, d), F32),
                   jax.ShapeDtypeStruct((n_e, ntp), F32),
                   jax.ShapeDtypeStruct((ntp * xpitch, LANES), F32)],
        scratch_shapes=[pltpu.VMEM(w_out.shape, BF16), pltpu.VMEM((2, ck, d), F32), pltpu.SemaphoreType.DMA((2,)),
                        pltpu.VMEM((TM, d), F32), pltpu.VMEM((TM, d), F32)],
        compiler_params=_cparams(("arbitrary",)),
        name="outproj",
    )(*lhs, x_all, *([mod_s, mod_p] * 3), ln_g.reshape(1, d), ln_b.reshape(1, d), w_router_t, w_out)


def _first_argmax(vals, iota, axis, size):
    m = jnp.max(vals, axis=axis, keepdims=True)
    idx = jnp.min(jnp.where(vals == m, iota, size), axis=axis, keepdims=True)
    return m, idx


def _route_kernel(lg_ref, bias_ref, e_ref, w_ref):
    n_e, tm = lg_ref.shape
    gsz = n_e // N_GROUPS
    neg = -jnp.inf
    scores = jax.nn.sigmoid(lg_ref[...])
    sel = scores + bias_ref[...]
    sel3 = sel.reshape(N_GROUPS, gsz, tm)
    io3 = lax.broadcasted_iota(jnp.int32, sel3.shape, 1)
    m1, i1 = _first_argmax(sel3, io3, 1, gsz)
    m2 = jnp.max(jnp.where(io3 == i1, neg, sel3), axis=1, keepdims=True)
    gscore = (m1 + m2).reshape(N_GROUPS, tm)
    iog = lax.broadcasted_iota(jnp.int32, gscore.shape, 0)
    gsel = jnp.zeros(gscore.shape, F32)
    cur = gscore
    for _ in range(TOPK_GROUPS):
        _, gi = _first_argmax(cur, iog, 0, N_GROUPS)
        pick = iog == gi
        gsel = jnp.where(pick, 1.0, gsel)
        cur = jnp.where(pick, neg, cur)
    esel = jnp.broadcast_to(gsel.reshape(N_GROUPS, 1, tm), sel3.shape).reshape(n_e, tm)
    ioe = lax.broadcasted_iota(jnp.int32, sel.shape, 0)
    iok = lax.broadcasted_iota(jnp.int32, (TOP_K, tm), 0)
    cur = jnp.where(esel > 0.5, sel, neg)
    eidx = jnp.zeros((TOP_K, tm), jnp.int32)
    sc = jnp.zeros((TOP_K, tm), F32)
    for r in range(TOP_K):
        _, ei = _first_argmax(cur, ioe, 0, n_e)
        pick = ioe == ei
        val = jnp.sum(jnp.where(pick, scores, 0.0), axis=0, keepdims=True)
        eidx = jnp.where(iok == r, ei, eidx)
        sc = jnp.where(iok == r, val, sc)
        cur = jnp.where(pick, neg, cur)
    e_ref[...] = eidx
    w_ref[...] = sc / jnp.sum(sc, axis=0, keepdims=True) * ROUTED_SCALE


def _route(logits_t, router_bias):
    n_e, ntp = logits_t.shape
    return pl.pallas_call(
        _route_kernel,
        grid=(ntp // TM,),
        in_specs=[pl.BlockSpec((n_e, TM), lambda i: (0, i)), pl.BlockSpec((n_e, 1), lambda i: (0, 0))],
        out_specs=[pl.BlockSpec((TOP_K, TM), lambda i: (0, i))] * 2,
        out_shape=[jax.ShapeDtypeStruct((TOP_K, ntp), jnp.int32), jax.ShapeDtypeStruct((TOP_K, ntp), F32)],
        compiler_params=_cparams(("arbitrary",)),
        name="route",
    )(logits_t, router_bias.reshape(n_e, 1))


def _dispatch_tables(eidx_t, n_e):
    flat_e = eidx_t.T.reshape(-1)
    nk = flat_e.shape[0]
    n_steps = nk // BM + n_e + 1
    abits = (nk + BM - 1).bit_length()
    experts = jnp.arange(n_e, dtype=jnp.int32)
    counts = jnp.sum((flat_e[:, None] == experts[None, :]).astype(jnp.int32), axis=0)
    nblk = (counts + BM - 1) // BM
    bends = jnp.cumsum(nblk)
    total = bends[-1]
    j = jnp.arange(BM, dtype=jnp.int32)
    real = jnp.left_shift(flat_e, abits) | jnp.arange(nk, dtype=jnp.int32)
    pad = jnp.where(j[None, :] < (nblk * BM - counts)[:, None],
                    jnp.left_shift(experts, abits)[:, None] | (nk + j)[None, :], jnp.iinfo(jnp.int32).max)
    ids = jnp.sort(jnp.concatenate([real, pad.reshape(-1)])) & ((1 << abits) - 1)
    table = jnp.concatenate([nk + j, ids] + [nk + j] * (NSLOT - 1))
    g = jnp.arange(n_steps, dtype=jnp.int32)
    e_of = jnp.sum((bends[None, :] <= g[:, None]).astype(jnp.int32), axis=1)
    e_last = jnp.sum((bends <= total - 1).astype(jnp.int32))
    blk_e = jnp.minimum(jnp.where(g < total, e_of, e_last), n_e - 1)
    return blk_e.astype(jnp.int32), total.reshape(1).astype(jnp.int32), table


def _experts_kernel(blk_e, total_ref, table, h2_hbm, wg_ref, wu_ref, wd_ref, y_hbm,
                    xbuf, ybuf, xb, hb, wg_bf, wu_bf, wd_bf, gsem, ssem, *, ntp, nc):
    g = pl.program_id(0)
    total = total_ref[0]
    s = g % NSLOT
    s1 = (g + 1) % NSLOT
    s2 = (g + 2) % NSLOT
    kbits = TOP_K.bit_length() - 1
    d = xb.shape[1]
    rpt = d // LANES
    pitch = ybuf.shape[1] // BM

    rpx = rpt
    xpitch = xbuf.shape[1] // BM

    def gather(j, slot, r):
        a = table[(j + 1) * BM + r]
        tok = jnp.minimum(lax.shift_right_logical(a, kbits), ntp - 1)
        return pltpu.make_async_copy(h2_hbm.at[pl.ds(tok * xpitch, rpx)], xbuf.at[slot, pl.ds(r * xpitch, rpx)],
                                     gsem.at[slot])

    def scatter(j, slot, r):
        a = table[(j + 1) * BM + r]
        return pltpu.make_async_copy(ybuf.at[slot, pl.ds(r * pitch, rpt)],
                                     y_hbm.at[pl.ds(pl.multiple_of(a * rpt, rpt), rpt)], ssem.at[slot])

    def wait_gather(slot):
        for r in range(BM):
            pltpu.make_async_copy(h2_hbm.at[pl.ds(0, rpx)], xbuf.at[slot, pl.ds(r * xpitch, rpx)], gsem.at[slot]).wait()

    def wait_scatter(slot):
        for r in range(BM):
            pltpu.make_async_copy(ybuf.at[slot, pl.ds(r * pitch, rpt)], y_hbm.at[pl.ds(0, rpt)], ssem.at[slot]).wait()

    @pl.when(g == 0)
    def _():
        ybuf[NSLOT - 1] = jnp.zeros(ybuf.shape[1:], F32)

        def body(r, c):
            gather(0, 0, r).start()
            gather(1, 1, r).start(priority=1)
            return c
        lax.fori_loop(0, BM, body, 0, unroll=8)

    @pl.when(g < total)
    def _():
        prev = blk_e[jnp.maximum(g - 1, 0)]

        @pl.when(jnp.logical_or(g == 0, blk_e[g] != prev))
        def _():
            wg_bf[...] = wg_ref[...].astype(BF16)
            wu_bf[...] = wu_ref[...].astype(BF16)
            wd_bf[...] = wd_ref[...].astype(BF16)

        wait_gather(s)

        @pl.when(g >= 2)
        def _():
            wait_scatter(s)

        d_e = wd_bf.shape[0]
        pieces = d_e // nc + d // nc
        rows = [list(range(p * BM // pieces, (p + 1) * BM // pieces)) for p in range(pieces)]

        def issue(p):
            for r in rows[p]:
                gather(g + 2, s2, r).start(priority=r % 2)
                scatter(g - 1, s2, r).start(priority=(r + 1) % 2)

        for c in range(rpx):
            xb[:, c * LANES:(c + 1) * LANES] = xbuf[s, pl.ds(c, BM, stride=xpitch), :].astype(BF16)
        p = 0
        for c in range(d_e // nc):
            cols = slice(c * nc, (c + 1) * nc)
            hg = jnp.dot(xb[...], wg_bf[:, cols], preferred_element_type=F32)
            hu = jnp.dot(xb[...], wu_bf[:, cols], preferred_element_type=F32)
            hb[:, cols] = (_silu(hg) * hu).astype(BF16)
            issue(p)
            p += 1
        for c in range(d // nc):
            y = jnp.dot(hb[...], wd_bf[:, c * nc:(c + 1) * nc], preferred_element_type=F32)
            for cc in range(nc // LANES):
                ybuf[s, pl.ds(c * (nc // LANES) + cc, BM, stride=pitch), :] = y[:, cc * LANES:(cc + 1) * LANES]
            issue(p)
            p += 1

    @pl.when(g == total)
    def _():
        wait_gather(s)
        wait_gather(s1)

        @pl.when(g >= 2)
        def _():
            wait_scatter(s)
        wait_scatter(s1)

        def body(r, c):
            scatter(g - 1, s2, r).start()
            return c
        lax.fori_loop(0, BM, body, 0, unroll=8)
        wait_scatter(s2)


def _experts(h2p, tables, w_gate, w_up, w_down, l):
    d, d_e = w_gate.shape[-2:]
    rpt = d // LANES
    pitch = _pitch(rpt)
    xpitch = _pitch(rpt)
    ntp = h2p.shape[0] // xpitch
    n_steps = tables[0].shape[0]
    wmap = lambda g, be, *_: (l, be[g], 0, 0)
    return pl.pallas_call(
        functools.partial(_experts_kernel, ntp=ntp, nc=256),
        grid_spec=pltpu.PrefetchScalarGridSpec(
            num_scalar_prefetch=3, grid=(n_steps,),
            in_specs=[pl.BlockSpec(memory_space=pl.ANY),
                      pl.BlockSpec((None, None, d, d_e), wmap),
                      pl.BlockSpec((None, None, d, d_e), wmap),
                      pl.BlockSpec((None, None, d_e, d), wmap)],
            out_specs=pl.BlockSpec(memory_space=pl.ANY),
            scratch_shapes=[pltpu.VMEM((NSLOT, BM * xpitch, LANES), F32),
                            pltpu.VMEM((NSLOT, BM * pitch, LANES), F32),
                            pltpu.VMEM((BM, d), BF16), pltpu.VMEM((BM, d_e), BF16),
                            pltpu.VMEM((d, d_e), BF16), pltpu.VMEM((d, d_e), BF16), pltpu.VMEM((d_e, d), BF16),
                            pltpu.SemaphoreType.DMA((NSLOT,)), pltpu.SemaphoreType.DMA((NSLOT,))]),
        out_shape=jax.ShapeDtypeStruct(((ntp * TOP_K + BM) * rpt, LANES), F32),
        compiler_params=_cparams(("arbitrary",)),
        name="experts",
    )(*tables, h2p, w_gate, w_up, w_down)


def _final_kernel(*refs, l, n_prompt_tiles, alpha, split):
    (y_hbm, w_ref, x1_ref, h2_ref, g2s, g2p, lng_ref, lnb_ref, wg_hbm, wu_hbm, wd_hbm) = refs[:11]
    if split:
        yp_ref, ysm_ref = refs[11:13]
        wg_bf, wu_bf, wd_bf, stage_a, stage_b, sem, ybuf, ysem, ffn_s, x2_s = refs[13:]
    else:
        x2_ref = refs[11]
        wg_bf, wu_bf, wd_bf, stage_a, stage_b, sem, ybuf, ysem, ffn_s = refs[12:]
    i = pl.program_id(0)
    tm, d = x1_ref.shape
    rpt = d // LANES
    rec = TOP_K * rpt
    pitch = ybuf.shape[1] // tm
    slot = i % 2

    def fetch(tile, sl, t):
        src = y_hbm.at[pl.ds(pl.multiple_of((tile * tm + t) * rec, rec), rec)]
        return pltpu.make_async_copy(src, ybuf.at[sl, pl.ds(t * pitch, rec)], ysem.at[sl])

    @pl.when(i == 0)
    def _():
        for t in range(tm):
            fetch(0, 0, t).start()
        _load_weight_bf16(wg_hbm.at[l], wg_bf, stage_a, sem, stage_a.shape[1])
        _load_weight_bf16(wu_hbm.at[l], wu_bf, stage_a, sem, stage_a.shape[1])
        _load_weight_bf16(wd_hbm.at[l], wd_bf, stage_b, sem, stage_b.shape[1])

    @pl.when(i + 1 < pl.num_programs(0))
    def _():
        for t in range(tm):
            fetch(i + 1, 1 - slot, t).start()

    hb = h2_ref[...].astype(BF16)
    sg = jnp.dot(hb, wg_bf[...], preferred_element_type=F32)
    su = jnp.dot(hb, wu_bf[...], preferred_element_type=F32)
    shared = jnp.dot((_silu(sg) * su).astype(BF16), wd_bf[...], preferred_element_type=F32)
    is_s = i >= n_prompt_tiles
    dst = x2_s if split else x2_ref

    for t in range(tm):
        fetch(0, slot, t).wait()
    w = w_ref[...]
    wk = [jnp.broadcast_to(w[:, k:k + 1], (tm, LANES)) for k in range(TOP_K)]
    for c in range(rpt):
        acc = shared[:, c * LANES:(c + 1) * LANES]
        for k in range(TOP_K):
            acc = acc + wk[k] * ybuf[slot, pl.ds(k * rpt + c, tm, stride=pitch), :]
        ffn_s[:, c * LANES:(c + 1) * LANES] = acc

    for r0 in range(0, tm, ROW_CHUNK):
        rs = slice(r0, r0 + ROW_CHUNK)
        g2 = jnp.where(is_s, g2s[rs, :], g2p[...])
        dst[rs, :] = _layer_norm(alpha * x1_ref[rs, :] + g2 * ffn_s[rs, :], lng_ref[...], lnb_ref[...])

    if split:
        @pl.when(i < n_prompt_tiles)
        def _():
            yp_ref[...] = x2_s[...]

        @pl.when(i == n_prompt_tiles)
        def _():
            ysm_ref[...] = x2_s[...]


def _final(yslots, wsel, x1, h2, mod_s, mod_p, l, ln_g, ln_b, ws_gate, ws_up, ws_down,
           n_prompt_rows, rows_per_batch, nb, alpha, split):
    ntp, d = x1.shape
    tm = TM_FINAL
    d_s = ws_gate.shape[-1]
    npt = n_prompt_rows // tm
    tpb = rows_per_batch // tm
    nt = ntp // tm
    tile = pl.BlockSpec((tm, d), lambda i: (i, 0))
    specs = [pl.BlockSpec(memory_space=pl.ANY), pl.BlockSpec((tm, TOP_K), lambda i: (i, 0)), tile, tile]
    specs += _mod_specs(l, 5, d, tm, npt, tpb, nb)
    specs += [pl.BlockSpec((1, d), lambda i: (0, 0))] * 2
    specs += [pl.BlockSpec(memory_space=pl.ANY)] * 3
    if split:
        out_specs = [pl.BlockSpec((tm, d), lambda i: (jnp.minimum(i, npt - 1), 0)),
                     pl.BlockSpec((tm, d), lambda i: (0, 0))]
        out_shape = [jax.ShapeDtypeStruct((n_prompt_rows, d), F32), jax.ShapeDtypeStruct((tm, d), F32)]
    else:
        out_specs = tile
        out_shape = jax.ShapeDtypeStruct((ntp, d), F32)
    ck = 256
    scratch = [pltpu.VMEM((d, d_s), BF16), pltpu.VMEM((d, d_s), BF16), pltpu.VMEM((d_s, d), BF16),
               pltpu.VMEM((2, ck, d_s), F32), pltpu.VMEM((2, ck, d), F32), pltpu.SemaphoreType.DMA((2,)),
               pltpu.VMEM((2, tm * _pitch(TOP_K * d // LANES), LANES), F32), pltpu.SemaphoreType.DMA((2,)),
               pltpu.VMEM((tm, d), F32)]
    if split:
        scratch.append(pltpu.VMEM((tm, d), F32))
    return pl.pallas_call(
        functools.partial(_final_kernel, l=l, n_prompt_tiles=npt, alpha=alpha, split=split),
        grid=(nt,),
        in_specs=specs, out_specs=out_specs, out_shape=out_shape,
        scratch_shapes=scratch,
        compiler_params=_cparams(("arbitrary",)),
        name="moe_final",
    )(yslots, wsel, x1, h2, mod_s, mod_p, ln_g.reshape(1, d), ln_b.reshape(1, d),
      ws_gate, ws_up, ws_down)


def kernel(x_prompt, x_sample, state_hgrn, state_pool, c_prompt, c_sample, mix0_w_in, mix0_w_out, hgrn_lb_table, hgrn_rms_w, pool_w, pool_scale, mix1_w_in, gmlp_ln_g, gmlp_ln_b, gmlp_w_sp, gmlp_b_sp, mix1_w_out, ada_w, ada_b, ln_mix_g, ln_mix_b, ln_ffn_g, ln_ffn_b, moe_w_router, moe_router_bias, moe_w_gate, moe_w_up, moe_w_down, shared_w_gate, shared_w_up, shared_w_down):
    nb, t, d = x_prompt.shape
    ns = x_sample.shape[0]
    depth = ada_w.shape[0]
    n_heads, dk = state_hgrn.shape[2], state_hgrn.shape[3]
    d_a = n_heads * dk
    d_b = state_pool.shape[3]
    n_hist = state_pool.shape[2]
    hist_pad = 16
    n_e = moe_w_router.shape[2]
    alpha = (2.0 * depth) ** 0.25
    n_p = nb * t
    npt = n_p // TM
    tpb = t // TM
    assert t % 512 == 0 and ns <= TM and ns % 8 == 0 and n_hist < hist_pad and d_a % d_b == 0

    x_all = jnp.concatenate([x_prompt.reshape(n_p, d), x_sample.reshape(ns, d), jnp.zeros((TM - ns, d), F32)], axis=0)
    c_all = jnp.concatenate([c_sample, jnp.zeros((TM - ns, d), F32), c_prompt, jnp.zeros((8 - nb % 8, d), F32)], axis=0)
    mod_s, mod_p = _adaln(c_all, ada_w, ada_b, TM)
    mod_p = mod_p.reshape(depth, mod_p.shape[1], 1, N_MOD * d)

    hgrn_p, pool_p, hgrn_s, pool_s, chunk_v = [], [], [], [], []
    x = x_all
    for l in range(depth):
        if l % 2 == 0:
            e = l // 2
            z = _inproj(x, mod_s, mod_p, l, mix0_w_in[e], npt, tpb, nb)
            o_a_p, s_p = _gla_prompt(z, hgrn_lb_table, hgrn_rms_w[e], l, nb, t, n_heads, dk)
            o_b_p, tail = _pool_prompt(z, pool_w[e], pool_scale[e], nb, t, d_b, 4 * d_a // d_b, hist_pad)
            hgrn_p.append(s_p)
            pool_p.append(tail[:, hist_pad - n_hist:])
            zs = z[n_p:n_p + ns]
            qt = zs[:, 0:d_a].reshape(ns, n_heads, dk).transpose(1, 2, 0)
            ft = zs[:, d_a:2 * d_a].reshape(ns, n_heads, dk).transpose(1, 2, 0)
            lbt_col = hgrn_lb_table.reshape(-1, n_heads, dk, 1)
            o_a_s, s_s = _gla_sample(qt, ft, lbt_col, zs[:, 2 * d_a:3 * d_a], zs[:, 3 * d_a:4 * d_a],
                                     hgrn_rms_w[e], state_hgrn[e], l)
            o_b_s, buf_s = _pool_sample(state_pool[e].reshape(ns, n_hist * d_b), z, pool_w[e], pool_scale[e],
                                        ns, d_b, n_p // ns, 4 * d_a // d_b, n_hist)
            hgrn_s.append(s_s)
            pool_s.append(buf_s.reshape(ns, n_hist, d_b))
            pad = jnp.zeros((TM - ns, d_a), BF16)
            o_a = jnp.concatenate([o_a_p, o_a_s.astype(BF16), pad], axis=0)
            o_b = jnp.concatenate([o_b_p, o_b_s, jnp.zeros((TM - ns, d_b), BF16)], axis=0)
            lhs, w_out = [o_a, o_b], mix0_w_out[e]
        else:
            o = l // 2
            u, v = _inproj(x, mod_s, mod_p, l, mix1_w_in[o], npt, tpb, nb, ln=(gmlp_ln_g[o], gmlp_ln_b[o]))
            chunk_v.append(v[n_p:n_p + ns].reshape(ns, 1, -1))
            lhs, w_out = [_spatial(u, v, gmlp_w_sp[o], gmlp_b_sp[o], npt)], mix1_w_out[o]
        x1, h2, logits_t, h2p = _outproj(lhs, x, mod_s, mod_p, l, w_out, ln_mix_g[l], ln_mix_b[l],
                                    moe_w_router[l].T, npt, tpb, nb, alpha)
        eidx_t, wsel_t = _route(logits_t, moe_router_bias[l])
        yslots = _experts(h2p, _dispatch_tables(eidx_t, n_e), moe_w_gate, moe_w_up, moe_w_down, l)
        last = l == depth - 1
        res = _final(yslots, wsel_t.T, x1, h2, mod_s, mod_p, l, ln_ffn_g[l], ln_ffn_b[l],
                     shared_w_gate, shared_w_up, shared_w_down, n_p, t, nb, alpha, split=last)
        if last:
            y_p, y_s = res
        else:
            x = res
    return (y_p.reshape(nb, t, d), y_s[:ns].reshape(ns, 1, d),
            jnp.stack(hgrn_p), jnp.stack(pool_p), jnp.stack(hgrn_s), jnp.stack(pool_s), jnp.stack(chunk_v))
```
